```python
import math, functools
import jax, jax.numpy as jnp
from jax import lax
import numpy as np

D_MODEL = 2048
BATCH = 2
SEQ = 4096
DEPTH = 4
DEC_BATCH = 128
DEC_SEQ = 4
PAST_LEN = 8192
PAGE_SIZE = 128

MLA_HEADS = 16
Q_LORA = 512
KV_LORA = 512
NOPE_DIM = 128
ROPE_DIM = 64
V_HEAD = 128
MLA_WIDTH = MLA_HEADS * V_HEAD
ROPE_THETA = 10000.0
MLA_SCALE = (NOPE_DIM + ROPE_DIM) ** -0.5
Q_BLOCK = 128
CONV_DIM = 1024
CONV_K = 3
GLA_HEADS = 4
GLA_DK = 128
GLA_DV = 256
GLA_KD = GLA_HEADS * GLA_DK
GLA_VD = GLA_HEADS * GLA_DV
GATE_RANK = 16
GLA_TAU = 16.0
GLA_CHUNK = 32
N_BRANCH = 3
MIX_WIDTH = MLA_WIDTH + CONV_DIM + GLA_VD
D_FF = 5632
EPS = 1e-6
IN_SIZES = (Q_LORA, KV_LORA, ROPE_DIM, CONV_DIM, CONV_DIM, CONV_DIM,
            GLA_KD, GLA_KD, GLA_VD, GATE_RANK, GLA_VD, N_BRANCH * D_MODEL)
IN_DIM = sum(IN_SIZES)

kernel_name = 'hybrid_mla_conv_gla_macaron_step'


def rmsnorm(x, g):
    xf = x.astype(jnp.float32)
    y = xf * lax.rsqrt(jnp.mean(xf * xf, axis=-1, keepdims=True) + EPS)
    return (y * g.astype(jnp.float32)).astype(x.dtype)


def rope(x, pos):
    half = ROPE_DIM // 2
    inv = ROPE_THETA ** (-jnp.arange(half, dtype=jnp.float32) / half)
    ang = pos.astype(jnp.float32)[:, None] * inv[None, :]
    cos = jnp.cos(ang)[None, :, None, :]
    sin = jnp.sin(ang)[None, :, None, :]
    xf = x.astype(jnp.float32)
    x1, x2 = xf[..., :half], xf[..., half:]
    return jnp.concatenate([x1 * cos - x2 * sin, x1 * sin + x2 * cos], axis=-1).astype(x.dtype)


def swiglu(x, wg, wu, wd):
    return (jax.nn.silu(x @ wg) * (x @ wu)) @ wd


def mla_attend_prompt(q_lat, q_pe, kv, pe):
    B, T, H, C = q_lat.shape
    nb = T // Q_BLOCK
    qb = q_lat.reshape(B, nb, Q_BLOCK, H, C).swapaxes(0, 1)
    pb = q_pe.reshape(B, nb, Q_BLOCK, H, ROPE_DIM).swapaxes(0, 1)
    qpos = jnp.arange(T, dtype=jnp.int32).reshape(nb, Q_BLOCK)
    kpos = jnp.arange(T, dtype=jnp.int32)

    def one_block(args):
        ql, qp, pq = args
        s = jnp.einsum('bqhc,bkc->bhqk', ql, kv) + jnp.einsum('bqhr,bkr->bhqk', qp, pe)
        s = s.astype(jnp.float32) * MLA_SCALE
        s = jnp.where(kpos[None, :] <= pq[:, None], s, -jnp.inf)
        p = jax.nn.softmax(s, axis=-1).astype(kv.dtype)
        return jnp.einsum('bhqk,bkc->bqhc', p, kv)

    o = lax.map(one_block, (qb, pb, qpos))
    return o.swapaxes(0, 1).reshape(B, T, H, C)


def mla_attend_sample(q_lat, q_pe, kv, pe, kv_past, pe_past):
    T = q_lat.shape[1]
    L = kv_past.shape[1]
    s_past = jnp.einsum('bqhc,bkc->bhqk', q_lat, kv_past) + jnp.einsum('bqhr,bkr->bhqk', q_pe, pe_past)
    s_new = jnp.einsum('bqhc,bkc->bhqk', q_lat, kv) + jnp.einsum('bqhr,bkr->bhqk', q_pe, pe)
    causal = jnp.tril(jnp.ones((T, T), dtype=bool))
    s_new = jnp.where(causal, s_new.astype(jnp.float32) * MLA_SCALE, -jnp.inf)
    s = jnp.concatenate([s_past.astype(jnp.float32) * MLA_SCALE, s_new], axis=-1)
    p = jax.nn.softmax(s, axis=-1).astype(kv.dtype)
    return (jnp.einsum('bhqk,bkc->bqhc', p[..., :L], kv_past)
            + jnp.einsum('bhqk,bkc->bqhc', p[..., L:], kv))


def short_conv(u, prev, w):
    T = u.shape[1]
    buf = jnp.concatenate([prev.astype(u.dtype), u], axis=1)
    y = sum(w[k] * buf[:, k:k + T] for k in range(CONV_K))
    return y, buf[:, T:]


def gla(q, k, v, logf, s0):
    B, T, H, DK = q.shape
    DV = v.shape[-1]
    C = math.gcd(T, GLA_CHUNK)
    N = T // C

    def chunks(a):
        return a.astype(jnp.float32).reshape(B, N, C, H, a.shape[-1]).transpose(1, 0, 3, 2, 4)

    qc = chunks(q) * (GLA_DK ** -0.5)
    kc, vc, gc = chunks(k), chunks(v), chunks(logf)
    b = jnp.cumsum(gc, axis=3)
    b_last = b[:, :, :, -1:, :]
    q_dec = qc * jnp.exp(b)
    k_dec = kc * jnp.exp(-b)
    k_to_end = kc * jnp.exp(b_last - b)
    causal = jnp.tril(jnp.ones((C, C), dtype=bool))
    a = jnp.where(causal, jnp.einsum('nbhid,nbhjd->nbhij', q_dec, k_dec), 0.0)
    o_intra = jnp.einsum('nbhij,nbhjv->nbhiv', a, vc)

    def step(S, xs):
        qd, kte, vv, bl = xs
        o_inter = jnp.einsum('bhid,bhdv->bhiv', qd, S)
        S = jnp.exp(bl)[..., 0, :, None] * S + jnp.einsum('bhjd,bhjv->bhdv', kte, vv)
        return S, o_inter

    s_final, o_inter = lax.scan(step, s0.astype(jnp.float32), (q_dec, k_to_end, vc, b_last))
    o = (o_intra + o_inter).transpose(1, 0, 3, 2, 4).reshape(B, T, H, DV)
    return o, s_final


def layer(x, pos, p, attend, conv_prev, gla_prev):
    B, T, _ = x.shape
    h = rmsnorm(x, p['norm_ffn1'])
    x = x + 0.5 * swiglu(h, p['ffn1_w_gate'], p['ffn1_w_up'], p['ffn1_w_down'])
    h = rmsnorm(x, p['norm_mix'])
    proj = h @ p['w_in']
    split_at = np.cumsum(IN_SIZES)[:-1].tolist()
    c_q, c_kv, k_pe, cb, cc, cv, gq, gk, gv, gg, gr, gates = jnp.split(proj, split_at, axis=-1)
    q = jnp.einsum('btc,chd->bthd', rmsnorm(c_q, p['q_norm']), p['w_uq'])
    q_nope = q[..., :NOPE_DIM]
    q_pe = rope(q[..., NOPE_DIM:], pos)
    q_lat = jnp.einsum('bthn,chn->bthc', q_nope, p['w_uk'])
    kv = rmsnorm(c_kv, p['kv_norm'])
    pe = rope(k_pe[:, :, None, :], pos)[:, :, 0, :]
    o_lat = attend(q_lat, q_pe, kv, pe)
    o_mla = jnp.einsum('bthc,chv->bthv', o_lat, p['w_uv']).reshape(B, T, MLA_WIDTH)
    y_conv, conv_new = short_conv(cc * cv, conv_prev, p['conv_w'])
    o_conv = cb * y_conv
    logf = jax.nn.log_sigmoid((gg @ p['gla_gate_w'] + p['gla_gate_b']).astype(jnp.float32)) / GLA_TAU
    o_g, gla_new = gla(gq.reshape(B, T, GLA_HEADS, GLA_DK), gk.reshape(B, T, GLA_HEADS, GLA_DK),
                       gv.reshape(B, T, GLA_HEADS, GLA_DV), logf.reshape(B, T, GLA_HEADS, GLA_DK), gla_prev)
    o_gla = rmsnorm(o_g.astype(x.dtype), p['gla_norm']).reshape(B, T, GLA_VD) * jax.nn.silu(gr)
    w_b = p['w_branch']
    branches = jnp.stack([o_mla @ w_b[:MLA_WIDTH],
                          o_conv @ w_b[MLA_WIDTH:MLA_WIDTH + CONV_DIM],
                          o_gla @ w_b[MLA_WIDTH + CONV_DIM:]], axis=2)
    g = jax.nn.sigmoid(gates.reshape(B, T, N_BRANCH, D_MODEL))
    x = x + jnp.sum(g * branches, axis=2) @ p['w_out']
    h = rmsnorm(x, p['norm_ffn2'])
    x = x + 0.5 * swiglu(h, p['ffn2_w_gate'], p['ffn2_w_up'], p['ffn2_w_down'])
    return x, kv, pe, conv_new.astype(conv_prev.dtype), gla_new.astype(gla_prev.dtype)


def setup_inputs(seed: int = 0) -> dict:
    key = jax.random.key(seed)
    keys = jax.random.split(key, 32)

    def nrm(i, shape, scale):
        return scale * jax.random.normal(keys[i], shape, jnp.float32)

    def gain(i, shape):
        return 1.0 + 0.05 * jax.random.normal(keys[i], shape, jnp.float32)

    n_pages = PAST_LEN // PAGE_SIZE
    n_used = DEC_BATCH * n_pages
    n_pool = n_used + n_used // 4
    page_table = jax.random.permutation(keys[6], n_pool)[:n_used].reshape(DEC_BATCH, n_pages).astype(jnp.int32)
    return {
        'x_prompt': nrm(0, (BATCH, SEQ, D_MODEL), 1.0),
        'x_sample': nrm(1, (DEC_BATCH, DEC_SEQ, D_MODEL), 1.0),
        'cache_kv': nrm(2, (DEPTH, n_pool, PAGE_SIZE, KV_LORA), 1.0),
        'cache_pe': nrm(3, (DEPTH, n_pool, PAGE_SIZE, ROPE_DIM), 1.0),
        'state_conv': nrm(4, (DEPTH, DEC_BATCH, CONV_K - 1, CONV_DIM), 1.0),
        'state_gla': nrm(5, (DEPTH, DEC_BATCH, GLA_HEADS, GLA_DK, GLA_DV), 1.0),
        'page_table': page_table,
        'norm_ffn1': gain(7, (DEPTH, D_MODEL)),
        'ffn1_w_gate': nrm(8, (DEPTH, D_MODEL, D_FF), D_MODEL ** -0.5),
        'ffn1_w_up': nrm(9, (DEPTH, D_MODEL, D_FF), D_MODEL ** -0.5),
        'ffn1_w_down': nrm(10, (DEPTH, D_FF, D_MODEL), D_FF ** -0.5),
        'norm_mix': gain(11, (DEPTH, D_MODEL)),
        'w_in': nrm(12, (DEPTH, D_MODEL, IN_DIM), D_MODEL ** -0.5),
        'q_norm': gain(13, (DEPTH, Q_LORA)),
        'w_uq': nrm(14, (DEPTH, Q_LORA, MLA_HEADS, NOPE_DIM + ROPE_DIM), Q_LORA ** -0.5),
        'kv_norm': gain(15, (DEPTH, KV_LORA)),
        'w_uk': nrm(16, (DEPTH, KV_LORA, MLA_HEADS, NOPE_DIM), KV_LORA ** -0.5),
        'w_uv': nrm(17, (DEPTH, KV_LORA, MLA_HEADS, V_HEAD), KV_LORA ** -0.5),
        'conv_w': nrm(18, (DEPTH, CONV_K, CONV_DIM), CONV_K ** -0.5),
        'gla_gate_w': nrm(19, (DEPTH, GATE_RANK, GLA_KD), GATE_RANK ** -0.5),
        'gla_gate_b': nrm(20, (DEPTH, GLA_KD), 0.1),
        'gla_norm': gain(21, (DEPTH, GLA_DV)),
        'w_branch': nrm(22, (DEPTH, MIX_WIDTH, D_MODEL), MIX_WIDTH ** -0.5),
        'w_out': nrm(23, (DEPTH, D_MODEL, D_MODEL), D_MODEL ** -0.5),
        'norm_ffn2': gain(24, (DEPTH, D_MODEL)),
        'ffn2_w_gate': nrm(25, (DEPTH, D_MODEL, D_FF), D_MODEL ** -0.5),
        'ffn2_w_up': nrm(26, (DEPTH, D_MODEL, D_FF), D_MODEL ** -0.5),
        'ffn2_w_down': nrm(27, (DEPTH, D_FF, D_MODEL), D_FF ** -0.5),
        'norm_final': gain(28, (D_MODEL,)),
    }


def reference(x_prompt, x_sample, cache_kv, cache_pe, state_conv, state_gla, page_table,
              norm_ffn1, ffn1_w_gate, ffn1_w_up, ffn1_w_down, norm_mix, w_in, q_norm, w_uq,
              kv_norm, w_uk, w_uv, conv_w, gla_gate_w, gla_gate_b, gla_norm, w_branch, w_out,
              norm_ffn2, ffn2_w_gate, ffn2_w_up, ffn2_w_down, norm_final):
    b_p, t_p = x_prompt.shape[0], x_prompt.shape[1]
    b_s, t_s = x_sample.shape[0], x_sample.shape[1]
    n_pages = page_table.shape[1]
    past_len = n_pages * cache_kv.shape[2]
    pos_p = jnp.arange(t_p, dtype=jnp.int32)
    pos_s = past_len + jnp.arange(t_s, dtype=jnp.int32)
    conv_zero = jnp.zeros((b_p, CONV_K - 1, CONV_DIM), state_conv.dtype)
    gla_zero = jnp.zeros((b_p, GLA_HEADS, GLA_DK, GLA_DV), state_gla.dtype)
    xp, xs = x_prompt, x_sample
    outs_p = ([], [], [], [])
    outs_s = ([], [], [], [])
    for l in range(DEPTH):
        p = dict(norm_ffn1=norm_ffn1[l], ffn1_w_gate=ffn1_w_gate[l], ffn1_w_up=ffn1_w_up[l],
                 ffn1_w_down=ffn1_w_down[l], norm_mix=norm_mix[l], w_in=w_in[l], q_norm=q_norm[l],
                 w_uq=w_uq[l], kv_norm=kv_norm[l], w_uk=w_uk[l], w_uv=w_uv[l], conv_w=conv_w[l],
                 gla_gate_w=gla_gate_w[l], gla_gate_b=gla_gate_b[l], gla_norm=gla_norm[l],
                 w_branch=w_branch[l], w_out=w_out[l], norm_ffn2=norm_ffn2[l],
                 ffn2_w_gate=ffn2_w_gate[l], ffn2_w_up=ffn2_w_up[l], ffn2_w_down=ffn2_w_down[l])
        xp, kv_p, pe_p, cv_p, st_p = layer(xp, pos_p, p, mla_attend_prompt, conv_zero, gla_zero)
        kv_past = cache_kv[l][page_table].reshape(b_s, past_len, KV_LORA)
        pe_past = cache_pe[l][page_table].reshape(b_s, past_len, ROPE_DIM)
        attend_s = functools.partial(mla_attend_sample, kv_past=kv_past, pe_past=pe_past)
        xs, kv_s, pe_s, cv_s, st_s = layer(xs, pos_s, p, attend_s, state_conv[l], state_gla[l])
        for acc, val in zip(outs_p, (kv_p, pe_p, cv_p, st_p)):
            acc.append(val)
        for acc, val in zip(outs_s, (kv_s, pe_s, cv_s, st_s)):
            acc.append(val)
    y_prompt = rmsnorm(xp, norm_final)
    y_sample = rmsnorm(xs, norm_final)
    new_kv_prompt = jnp.stack(outs_p[0])
    new_pe_prompt = jnp.stack(outs_p[1])
    new_conv_prompt = jnp.stack(outs_p[2])
    new_gla_prompt = jnp.stack(outs_p[3])
    new_kv_sample = jnp.stack(outs_s[0])
    new_pe_sample = jnp.stack(outs_s[1])
    new_conv_sample = jnp.stack(outs_s[2])
    new_gla_sample = jnp.stack(outs_s[3])
    return (y_prompt, y_sample, new_kv_prompt, new_pe_prompt, new_conv_prompt, new_gla_prompt,
            new_kv_sample, new_pe_sample, new_conv_sample, new_gla_sample)
```

```python
import functools
import math

import jax
import jax.numpy as jnp
from jax import lax
from jax.experimental import pallas as pl
from jax.experimental.pallas import tpu as pltpu

F32 = jnp.float32
BF16 = jnp.bfloat16

EPS = 1e-6
ROPE_THETA = 10000.0
GLA_TAU = 16.0
GLA_CHUNK = 32
LANES = 128
SUBLANES = 8
VMEM_LIMIT = 56 * 1024 * 1024

NT_DIMS = (((1,), (1,)), ((), ()))


def _div_tile(n, target, mult):
    for t in range(min(n, target), 0, -1):
        if n % t == 0 and t % mult == 0:
            return t
    return n


def _params(*sem):
    return pltpu.CompilerParams(dimension_semantics=sem, vmem_limit_bytes=VMEM_LIMIT)


def _rms(x, g):
    return x * lax.rsqrt(jnp.mean(x * x, axis=-1, keepdims=True) + EPS) * g


def _dot(a, b):
    return jnp.dot(a, b, preferred_element_type=F32)


def _dot_nt(a, b):
    return lax.dot_general(a, b, NT_DIMS, preferred_element_type=F32)


def _log_sigmoid(x):
    return jnp.minimum(x, 0.0) - jnp.log1p(jnp.exp(-jnp.abs(x)))


def _rmsnorm_body(x_ref, g_ref, o_ref):
    o_ref[...] = _rms(x_ref[...], g_ref[...]).astype(o_ref.dtype)


def rmsnorm_call(x, g):
    m, d = x.shape
    tm = _div_tile(m, 512, 16)
    return pl.pallas_call(
        _rmsnorm_body,
        grid=(m // tm,),
        in_specs=[pl.BlockSpec((tm, d), lambda i: (i, 0)),
                  pl.BlockSpec((1, d), lambda i: (0, 0))],
        out_specs=pl.BlockSpec((tm, d), lambda i: (i, 0)),
        out_shape=jax.ShapeDtypeStruct((m, d), BF16),
        compiler_params=_params("parallel"),
        name="rmsnorm",
    )(x, g.reshape(1, d))


def _ffn_body(nf, final, x_ref, h_ref, wg_ref, wu_ref, wd_ref, g_ref, *rest):
    if final:
        y_ref, acc_ref = rest
    else:
        xo_ref, hn_ref, acc_ref = rest
    f = pl.program_id(1)

    @pl.when(f == 0)
    def _():
        acc_ref[...] = jnp.zeros_like(acc_ref)

    h = h_ref[...]
    gate = _dot(h, wg_ref[...])
    up = _dot(h, wu_ref[...])
    act = (gate * jax.nn.sigmoid(gate)) * up
    acc_ref[...] += _dot(act.astype(BF16), wd_ref[...])

    @pl.when(f == nf - 1)
    def _():
        xo = x_ref[...] + 0.5 * acc_ref[...]
        if final:
            y_ref[...] = _rms(xo, g_ref[...])
        else:
            xo_ref[...] = xo
            hn_ref[...] = _rms(xo, g_ref[...]).astype(BF16)


def ffn_call(x, h, wg, wu, wd, g_next, final):
    m, d = x.shape
    ff = wg.shape[1]
    tm = _div_tile(m, 544, 16)
    tf = _div_tile(ff, 512, LANES)
    nf = ff // tf
    row = pl.BlockSpec((tm, d), lambda i, f: (i, 0))
    if final:
        out_shape = jax.ShapeDtypeStruct((m, d), F32)
        out_specs = row
    else:
        out_shape = (jax.ShapeDtypeStruct((m, d), F32), jax.ShapeDtypeStruct((m, d), BF16))
        out_specs = (row, row)
    return pl.pallas_call(
        functools.partial(_ffn_body, nf, final),
        grid=(m // tm, nf),
        in_specs=[row, row,
                  pl.BlockSpec((d, tf), lambda i, f: (0, f)),
                  pl.BlockSpec((d, tf), lambda i, f: (0, f)),
                  pl.BlockSpec((tf, d), lambda i, f: (f, 0)),
                  pl.BlockSpec((1, d), lambda i, f: (0, 0))],
        out_specs=out_specs,
        out_shape=out_shape,
        scratch_shapes=[pltpu.VMEM((tm, d), F32)],
        compiler_params=_params("parallel", "arbitrary"),
        name="ffn",
    )(x, h, wg, wu, wd, g_next.reshape(1, d))


class InLayout:
    def __init__(self, q_lora, kv_lora, rope, conv, gla_kd, gla_vd, rank, n_gate):
        self.tn = 512
        assert q_lora == self.tn and kv_lora == self.tn and 2 * rope == LANES and rank <= LANES
        self.q_lora, self.kv_lora, self.rope = q_lora, kv_lora, rope
        self.conv, self.gla_kd, self.gla_vd, self.rank, self.n_gate = conv, gla_kd, gla_vd, rank, n_gate
        off = 0
        self.cq = off; off += q_lora
        self.ckv = off; off += kv_lora
        self.small = off; off += self.tn
        self.cb = off; off += conv
        self.cc = off; off += conv
        self.cv = off; off += conv
        self.gq = off; off += gla_kd
        self.gk = off; off += gla_kd
        self.gv = off; off += gla_vd
        self.gr = off; off += gla_vd
        self.gates = off; off += n_gate
        self.width = off
        assert self.width % self.tn == 0

    def arrange(self, w_in):
        sizes = (self.q_lora, self.kv_lora, self.rope, self.conv, self.conv, self.conv,
                 self.gla_kd, self.gla_kd, self.gla_vd, self.rank, self.gla_vd, self.n_gate)
        assert sum(sizes) == w_in.shape[1]
        starts = [0]
        for s in sizes:
            starts.append(starts[-1] + s)
        c_q, c_kv, k_pe, cb, cc, cv, gq, gk, gv, gg, gr, gates = [
            w_in[:, starts[i]:starts[i + 1]] for i in range(len(sizes))]
        half = self.rope // 2
        k_pe_sw = jnp.concatenate([k_pe[:, half:], k_pe[:, :half]], axis=1)
        pad = jnp.zeros((w_in.shape[0], self.tn - 2 * self.rope - self.rank), w_in.dtype)
        return jnp.concatenate([c_q, c_kv, k_pe, k_pe_sw, gg, pad, cb, cc, cv, gq, gk, gv, gr, gates], axis=1)


def _inproj_body(rope, h_ref, w_ref, qg_ref, kvg_ref, tab_ref, proj_ref, qn_ref, kv_ref, pe_ref, kcat_ref):
    n = pl.program_id(1)
    proj_ref[...] = _dot(h_ref[...], w_ref[...])

    @pl.when(n == 0)
    def _():
        qn_ref[...] = _rms(proj_ref[...], qg_ref[...]).astype(BF16)

    @pl.when(n == 1)
    def _():
        kv = _rms(proj_ref[...], kvg_ref[...])
        kv_ref[...] = kv
        kcat_ref[:, :kv.shape[1]] = kv.astype(BF16)

    @pl.when(n == 2)
    def _():
        prod = proj_ref[:, :LANES] * tab_ref[...]
        pe = prod + pltpu.roll(prod, rope, 1)
        lane = lax.broadcasted_iota(jnp.int32, pe.shape, 1)
        pe_ref[...] = pe[:, :rope]
        kcat_ref[:, kv_ref.shape[1]:] = jnp.where(lane < rope, pe, 0.0).astype(BF16)


def inproj_call(h, w_al, q_norm, kv_norm, tab, lay):
    m, d = h.shape
    tn = lay.tn
    tm = _div_tile(m, 1088, 16)
    kw = lay.kv_lora + LANES
    row = lambda width: pl.BlockSpec((tm, width), lambda i, n: (i, 0))
    return pl.pallas_call(
        functools.partial(_inproj_body, lay.rope),
        grid=(m // tm, lay.width // tn),
        in_specs=[row(d),
                  pl.BlockSpec((d, tn), lambda i, n: (0, n)),
                  pl.BlockSpec((1, tn), lambda i, n: (0, 0)),
                  pl.BlockSpec((1, tn), lambda i, n: (0, 0)),
                  row(LANES)],
        out_specs=(pl.BlockSpec((tm, tn), lambda i, n: (i, n)),
                   row(tn), row(tn), row(lay.rope), row(kw)),
        out_shape=(jax.ShapeDtypeStruct((m, lay.width), F32),
                   jax.ShapeDtypeStruct((m, tn), BF16),
                   jax.ShapeDtypeStruct((m, tn), F32),
                   jax.ShapeDtypeStruct((m, lay.rope), F32),
                   jax.ShapeDtypeStruct((m, kw), BF16)),
        compiler_params=_params("parallel", "arbitrary"),
        name="inproj",
    )(h, w_al, q_norm.reshape(1, tn), kv_norm.reshape(1, tn), tab)


def _qproj_body(heads, nope, rope, qn_ref, wn_ref, wp_ref, wps_ref, wuk_ref, tab_ref, q_ref):
    qn = qn_ref[...]
    tab = tab_ref[...]
    tab_s = pltpu.roll(tab, rope, 1)
    q_nope = _dot(qn, wn_ref[...]).astype(BF16)
    q_pe = _dot(qn, wp_ref[...])
    q_pe_sw = _dot(qn, wps_ref[...])
    lat = wuk_ref.shape[2]
    for hd in range(heads):
        q_ref[hd, :, :lat] = _dot(q_nope[:, hd * nope:(hd + 1) * nope], wuk_ref[hd]).astype(BF16)
        sl = slice(hd * LANES, (hd + 1) * LANES)
        q_ref[hd, :, lat:] = (q_pe[:, sl] * tab + q_pe_sw[:, sl] * tab_s).astype(BF16)


def qproj_call(qn, wn, wp, wps, wuk_t, tab):
    m, ql = qn.shape
    heads, nope, lat = wuk_t.shape
    rope = LANES // 2
    tm = _div_tile(m, 272, 16)
    full = lambda a: pl.BlockSpec(a.shape, lambda i: (0,) * a.ndim)
    return pl.pallas_call(
        functools.partial(_qproj_body, heads, nope, rope),
        grid=(m // tm,),
        in_specs=[pl.BlockSpec((tm, ql), lambda i: (i, 0)),
                  full(wn), full(wp), full(wps), full(wuk_t),
                  pl.BlockSpec((tm, LANES), lambda i: (i, 0))],
        out_specs=pl.BlockSpec((heads, tm, lat + LANES), lambda i: (0, i, 0)),
        out_shape=jax.ShapeDtypeStruct((heads, m, lat + LANES), BF16),
        compiler_params=_params("parallel"),
        name="qproj",
    )(qn, wn, wp, wps, wuk_t, tab)


def _attn_p_body(tq, tk, nk, heads, scale, q_ref, k_ref, wuv_ref, o_ref, m_ref, l_ref, acc_ref):
    qi = pl.program_id(1)
    ki = pl.program_id(2)
    rows = heads * tq
    lat = acc_ref.shape[1]
    k_last = (qi * tq + tq - 1) // tk

    @pl.when(ki == 0)
    def _():
        m_ref[...] = jnp.full_like(m_ref, -jnp.inf)
        l_ref[...] = jnp.zeros_like(l_ref)
        acc_ref[...] = jnp.zeros_like(acc_ref)

    def step(masked):
        q = q_ref[...].reshape(rows, q_ref.shape[2])
        k = k_ref[...]
        s = _dot_nt(q, k) * scale
        if masked:
            r = lax.broadcasted_iota(jnp.int32, s.shape, 0)
            c = lax.broadcasted_iota(jnp.int32, s.shape, 1)
            s = jnp.where(ki * tk + c <= qi * tq + lax.rem(r, tq), s, -jnp.inf)
        m_prev = m_ref[...]
        m_new = jnp.maximum(m_prev, jnp.max(s, axis=-1, keepdims=True))
        alpha = jnp.exp(m_prev - m_new)
        p = jnp.exp(s - m_new)
        l_ref[...] = alpha * l_ref[...] + jnp.sum(p, axis=-1, keepdims=True)
        acc_ref[...] = alpha * acc_ref[...] + _dot(p.astype(BF16), k[:, :lat])
        m_ref[...] = m_new

    @pl.when(ki < k_last)
    def _():
        step(False)

    @pl.when(ki == k_last)
    def _():
        step(True)

    @pl.when(ki == nk - 1)
    def _():
        o = (acc_ref[...] / l_ref[...]).astype(BF16)
        vh = wuv_ref.shape[2]
        for hd in range(heads):
            o_ref[:, hd * vh:(hd + 1) * vh] = _dot(o[hd * tq:(hd + 1) * tq], wuv_ref[hd]).astype(BF16)


def attn_prompt_call(q, kcat, wuv, batch, seq, scale):
    heads, _, qw = q.shape
    lat, vh = wuv.shape[1], wuv.shape[2]
    tq = _div_tile(seq, 128, 16)
    tk = _div_tile(seq, 512, tq)
    assert tk % tq == 0
    nq, nk = seq // tq, seq // tk
    return pl.pallas_call(
        functools.partial(_attn_p_body, tq, tk, nk, heads, scale),
        grid=(batch, nq, nk),
        in_specs=[pl.BlockSpec((heads, tq, qw), lambda b, i, j: (0, b * nq + i, 0)),
                  pl.BlockSpec((tk, qw), lambda b, i, j: (b * nk + jnp.minimum(j, (i * tq + tq - 1) // tk), 0)),
                  pl.BlockSpec(wuv.shape, lambda b, i, j: (0, 0, 0))],
        out_specs=pl.BlockSpec((tq, heads * vh), lambda b, i, j: (b * nq + i, 0)),
        out_shape=jax.ShapeDtypeStruct((batch * seq, heads * vh), BF16),
        scratch_shapes=[pltpu.VMEM((heads * tq, 1), F32), pltpu.VMEM((heads * tq, 1), F32),
                        pltpu.VMEM((heads * tq, lat), F32)],
        compiler_params=_params("parallel", "parallel", "arbitrary"),
        name="attn_prompt",
    )(q, kcat, wuv)


def _attn_s_body(pages, ts, nj, scale, pt_ref, q_ref, kn_ref, *refs):
    del pt_ref
    kv_refs, pe_refs = refs[:pages], refs[pages:2 * pages]
    o_ref, m_ref, l_ref, acc_ref = refs[2 * pages:]
    j = pl.program_id(1)
    lat = acc_ref.shape[1]
    rope = pe_refs[0].shape[1]
    q = q_ref[...]

    @pl.when(j == 0)
    def _():
        qf = q.astype(F32)
        kn = kn_ref[...].astype(F32)
        t_row = lax.rem(lax.broadcasted_iota(jnp.int32, (q.shape[0], 1), 0), ts)
        sc = []
        for t in range(ts):
            s_t = jnp.sum(qf * kn[t:t + 1, :], axis=-1, keepdims=True) * scale
            sc.append(jnp.where(t <= t_row, s_t, -jnp.inf))
        m = functools.reduce(jnp.maximum, sc)
        l = jnp.zeros_like(m)
        acc = jnp.zeros((q.shape[0], lat), F32)
        for t in range(ts):
            p_t = jnp.exp(sc[t] - m)
            l = l + p_t
            acc = acc + p_t.astype(BF16).astype(F32) * kn[t:t + 1, :lat]
        m_ref[...] = m
        l_ref[...] = l
        acc_ref[...] = acc

    kvs = [r[...].astype(BF16) for r in kv_refs]
    s = jnp.concatenate(
        [_dot_nt(q[:, :lat], kvs[i]) + _dot_nt(q[:, lat:lat + rope], pe_refs[i][...].astype(BF16))
         for i in range(pages)], axis=1) * scale
    m_prev = m_ref[...]
    m_new = jnp.maximum(m_prev, jnp.max(s, axis=-1, keepdims=True))
    alpha = jnp.exp(m_prev - m_new)
    p = jnp.exp(s - m_new)
    l_ref[...] = alpha * l_ref[...] + jnp.sum(p, axis=-1, keepdims=True)
    p = p.astype(BF16)
    page = kvs[0].shape[0]
    pv = _dot(p[:, :page], kvs[0])
    for i in range(1, pages):
        pv = pv + _dot(p[:, i * page:(i + 1) * page], kvs[i])
    acc_ref[...] = alpha * acc_ref[...] + pv
    m_ref[...] = m_new

    @pl.when(j == nj - 1)
    def _():
        o_ref[...] = (acc_ref[...] / l_ref[...]).astype(BF16)


def attn_sample_call(qs, kn, cache_kv, cache_pe, page_table, layer, scale):
    bs, rows, qw = qs.shape
    ts = kn.shape[1]
    n_pages = page_table.shape[1]
    page, lat = cache_kv.shape[2], cache_kv.shape[3]
    rope = cache_pe.shape[3]
    pages = _div_tile(n_pages, 8, 1)
    nj = n_pages // pages

    def page_spec(width, i):
        return pl.BlockSpec((None, None, page, width),
                            lambda b, j, pt: (layer, pt[b, j * pages + i], 0, 0))

    grid_spec = pltpu.PrefetchScalarGridSpec(
        num_scalar_prefetch=1,
        grid=(bs, nj),
        in_specs=[pl.BlockSpec((None, rows, qw), lambda b, j, pt: (b, 0, 0)),
                  pl.BlockSpec((None, ts, qw), lambda b, j, pt: (b, 0, 0))]
                 + [page_spec(lat, i) for i in range(pages)]
                 + [page_spec(rope, i) for i in range(pages)],
        out_specs=pl.BlockSpec((None, rows, lat), lambda b, j, pt: (b, 0, 0)),
        scratch_shapes=[pltpu.VMEM((rows, 1), F32), pltpu.VMEM((rows, 1), F32), pltpu.VMEM((rows, lat), F32)],
    )
    return pl.pallas_call(
        functools.partial(_attn_s_body, pages, ts, nj, scale),
        grid_spec=grid_spec,
        out_shape=jax.ShapeDtypeStruct((bs, rows, lat), BF16),
        compiler_params=_params("parallel", "arbitrary"),
        name="attn_sample",
    )(page_table, qs, kn, *([cache_kv] * pages), *([cache_pe] * pages))


def _uv_body(o_ref, w_ref, out_ref):
    out_ref[...] = _dot(o_ref[...], w_ref[...]).astype(BF16)


def uv_call(o_t, wuv):
    heads, m, lat = o_t.shape
    vh = wuv.shape[2]
    return pl.pallas_call(
        _uv_body,
        grid=(heads,),
        in_specs=[pl.BlockSpec((None, m, lat), lambda h: (h, 0, 0)),
                  pl.BlockSpec((None, lat, vh), lambda h: (h, 0, 0))],
        out_specs=pl.BlockSpec((m, vh), lambda h: (0, h)),
        out_shape=jax.ShapeDtypeStruct((m, heads * vh), BF16),
        compiler_params=_params("parallel"),
        name="uv_sample",
    )(o_t, wuv)


def _conv_taps(w_ref, u2, u1, u0):
    return w_ref[0:1, :] * u2 + w_ref[1:2, :] * u1 + w_ref[2:3, :] * u0


def _conv_p_body(nt, cb_ref, cc_ref, cv_ref, w_ref, o_ref, st_ref, carry_ref):
    ti = pl.program_id(2)

    @pl.when(ti == 0)
    def _():
        carry_ref[...] = jnp.zeros_like(carry_ref)

    u = cc_ref[...] * cv_ref[...]
    tt = u.shape[0]
    c1 = carry_ref[SUBLANES - 1:SUBLANES, :]
    c2 = carry_ref[SUBLANES - 2:SUBLANES - 1, :]
    row = lax.broadcasted_iota(jnp.int32, u.shape, 0)
    u1 = jnp.where(row == 0, c1, pltpu.roll(u, 1, 0))
    u2 = jnp.where(row == 0, c2, jnp.where(row == 1, c1, pltpu.roll(u, 2, 0)))
    o_ref[...] = (cb_ref[...] * _conv_taps(w_ref, u2, u1, u)).astype(BF16)
    carry_ref[...] = u[tt - SUBLANES:, :]

    @pl.when(ti == nt - 1)
    def _():
        st_ref[...] = u[tt - 2:, :]


def conv_prompt_call(proj, conv_w, lay, batch, seq):
    cdim = lay.conv
    tc = 512
    tt = _div_tile(seq, 512, SUBLANES)
    nt = seq // tt
    col = lambda off: pl.BlockSpec((tt, tc), lambda b, c, t: (b * nt + t, off // tc + c))
    return pl.pallas_call(
        functools.partial(_conv_p_body, nt),
        grid=(batch, cdim // tc, nt),
        in_specs=[col(lay.cb), col(lay.cc), col(lay.cv),
                  pl.BlockSpec((conv_w.shape[0], tc), lambda b, c, t: (0, c))],
        out_specs=(pl.BlockSpec((tt, tc), lambda b, c, t: (b * nt + t, c)),
                   pl.BlockSpec((None, 2, tc), lambda b, c, t: (b, 0, c))),
        out_shape=(jax.ShapeDtypeStruct((batch * seq, cdim), BF16),
                   jax.ShapeDtypeStruct((batch, 2, cdim), F32)),
        scratch_shapes=[pltpu.VMEM((SUBLANES, tc), F32)],
        compiler_params=_params("parallel", "parallel", "arbitrary"),
        name="conv_prompt",
    )(proj, proj, proj, conv_w)


def _conv_s_body(ts, cb_ref, cc_ref, cv_ref, w_ref, p0_ref, p1_ref, o_ref, u_ref):
    u = cc_ref[...] * cv_ref[...]
    p1 = p1_ref[...]
    tok = lax.rem(lax.broadcasted_iota(jnp.int32, u.shape, 0), ts)
    u1 = jnp.where(tok == 0, p1, pltpu.roll(u, 1, 0))
    u2 = jnp.where(tok == 0, p0_ref[...], jnp.where(tok == 1, p1, pltpu.roll(u, 2, 0)))
    o_ref[...] = (cb_ref[...] * _conv_taps(w_ref, u2, u1, u)).astype(BF16)
    u_ref[...] = u


def conv_sample_call(proj, conv_w, state_rows, lay, row0, ts):
    cdim = lay.conv
    tc = 512
    ms = state_rows.shape[0]
    assert row0 % ms == 0
    rb = row0 // ms
    nc = cdim // tc
    col = lambda off: pl.BlockSpec((ms, tc), lambda c: (rb, off // tc + c))
    st = lambda k: pl.BlockSpec((ms, tc), lambda c: (0, k * nc + c))
    out = pl.BlockSpec((ms, tc), lambda c: (0, c))
    return pl.pallas_call(
        functools.partial(_conv_s_body, ts),
        grid=(nc,),
        in_specs=[col(lay.cb), col(lay.cc), col(lay.cv),
                  pl.BlockSpec((conv_w.shape[0], tc), lambda c: (0, c)), st(0), st(1)],
        out_specs=(out, out),
        out_shape=(jax.ShapeDtypeStruct((ms, cdim), BF16), jax.ShapeDtypeStruct((ms, cdim), F32)),
        compiler_params=_params("parallel"),
        name="conv_sample",
    )(proj, proj, proj, conv_w, state_rows, state_rows)


def _seg_cumsum(x, pos, seg):
    d = 1
    while d < seg:
        x = x + jnp.where(pos >= d, pltpu.roll(x, d, 0), 0.0)
        d *= 2
    return x


def _gla_gates(gg_ref, gw_ref, gb_ref):
    lg = _dot(gg_ref[...].astype(BF16), gw_ref[...]) + gb_ref[...]
    return _log_sigmoid(lg) / GLA_TAU


def _gla_out(o, gn_ref, gr_ref):
    gr = gr_ref[...]
    return (_rms(o, gn_ref[...]) * (gr * jax.nn.sigmoid(gr))).astype(BF16)


def _col(xt, r, width):
    return jnp.broadcast_to(xt[:, r:r + 1], (xt.shape[0], width))


def _gla_p_body(nt, chunk, gq_ref, gk_ref, gv_ref, gr_ref, gg_ref, gw_ref, gb_ref, gn_ref, o_ref, so_ref, s_ref):
    ti = pl.program_id(2)

    @pl.when(ti == 0)
    def _():
        s_ref[...] = jnp.zeros_like(s_ref)

    tt, dk = gq_ref.shape
    dv = gv_ref.shape[1]
    nch = tt // chunk
    logf = _gla_gates(gg_ref, gw_ref, gb_ref)
    row = lax.broadcasted_iota(jnp.int32, (tt, dk), 0)
    b = _seg_cumsum(logf, lax.rem(row, chunk), chunk)
    b3 = b.reshape(nch, chunk, dk)
    bl = jnp.broadcast_to(b3[:, chunk - 1:chunk, :], b3.shape).reshape(tt, dk)
    k = gk_ref[...]
    q_dec = (gq_ref[...] * (dk ** -0.5) * jnp.exp(b)).astype(BF16)
    k_dec = (k * jnp.exp(-b)).astype(BF16)
    k_end = k * jnp.exp(bl - b)
    v = gv_ref[...].astype(BF16)

    r2 = lax.broadcasted_iota(jnp.int32, (tt, tt), 0)
    c2 = lax.broadcasted_iota(jnp.int32, (tt, tt), 1)
    same = (r2 // chunk == c2 // chunk) & (c2 <= r2)
    a = jnp.where(same, _dot_nt(q_dec, k_dec), 0.0)
    o = _dot(a.astype(BF16), v)

    k_end_t = k_end.T.astype(BF16)
    decay_t = jnp.exp(bl).T
    tok = lax.broadcasted_iota(jnp.int32, (dk, tt), 1)
    s = s_ref[...]
    parts = []
    for c in range(nch):
        lo = c * chunk
        parts.append(_dot(q_dec[lo:lo + chunk], s.astype(BF16)))
        k_c = jnp.where((tok >= lo) & (tok < lo + chunk), k_end_t, jnp.zeros_like(k_end_t))
        s = _col(decay_t, lo, dv) * s + _dot(k_c, v)
    s_ref[...] = s
    o = o + jnp.concatenate(parts, axis=0)
    o_ref[...] = _gla_out(o, gn_ref, gr_ref)

    @pl.when(ti == nt - 1)
    def _():
        so_ref[...] = s


def gla_prompt_call(proj, gate_w, gate_b, gla_norm, lay, batch, seq, heads):
    dk = lay.gla_kd // heads
    dv = lay.gla_vd // heads
    chunk = math.gcd(seq, GLA_CHUNK)
    tt = _div_tile(seq, 256, chunk)
    assert tt % LANES == 0 and dk % LANES == 0
    nt = seq // tt
    colk = lambda off: pl.BlockSpec((tt, dk), lambda b, h, t: (b * nt + t, off // dk + h))
    colv = lambda off: pl.BlockSpec((tt, dv), lambda b, h, t: (b * nt + t, off // dv + h))
    return pl.pallas_call(
        functools.partial(_gla_p_body, nt, chunk),
        grid=(batch, heads, nt),
        in_specs=[colk(lay.gq), colk(lay.gk), colv(lay.gv), colv(lay.gr),
                  pl.BlockSpec((tt, LANES), lambda b, h, t: (b * nt + t, lay.small // LANES + 1)),
                  pl.BlockSpec((LANES, dk), lambda b, h, t: (0, h)),
                  pl.BlockSpec((1, dk), lambda b, h, t: (0, h)),
                  pl.BlockSpec((1, dv), lambda b, h, t: (0, 0))],
        out_specs=(pl.BlockSpec((tt, dv), lambda b, h, t: (b * nt + t, h)),
                   pl.BlockSpec((None, None, dk, dv), lambda b, h, t: (b, h, 0, 0))),
        out_shape=(jax.ShapeDtypeStruct((batch * seq, lay.gla_vd), BF16),
                   jax.ShapeDtypeStruct((batch, heads, dk, dv), F32)),
        scratch_shapes=[pltpu.VMEM((dk, dv), F32)],
        compiler_params=_params("parallel", "parallel", "arbitrary"),
        name="gla_prompt",
    )(proj, proj, proj, proj, proj, gate_w, gate_b, gla_norm)


def _gla_s_body(ts, gq_ref, gk_ref, gv_ref, gr_ref, gg_ref, gw_ref, gb_ref, gn_ref, s0_ref, o_ref, so_ref):
    rt, dk = gq_ref.shape
    dv = gv_ref.shape[1]
    per = SUBLANES // ts
    logf = _gla_gates(gg_ref, gw_ref, gb_ref)
    row = lax.broadcasted_iota(jnp.int32, (rt, dk), 0)
    tok = lax.rem(row, ts)
    b = _seg_cumsum(logf, tok, ts)
    bl = b
    for d in range(1, ts):
        bl = jnp.where(tok == ts - 1 - d, pltpu.roll(b, rt - d, 0), bl)
    k = gk_ref[...]
    q_dec = (gq_ref[...] * (dk ** -0.5) * jnp.exp(b)).astype(BF16)
    k_dec = (k * jnp.exp(-b)).astype(BF16)
    k_end = (k * jnp.exp(bl - b)).astype(BF16).astype(F32)
    v16 = gv_ref[...].astype(BF16)
    vf = v16.astype(F32)

    r2 = lax.broadcasted_iota(jnp.int32, (rt, rt), 0)
    c2 = lax.broadcasted_iota(jnp.int32, (rt, rt), 1)
    same = (r2 // ts == c2 // ts) & (c2 <= r2)
    a = jnp.where(same, _dot_nt(q_dec, k_dec), 0.0)
    o = _dot(a.astype(BF16), v16)

    zpad = jnp.zeros((LANES - rt, dk), F32)
    k_end_t = jnp.concatenate([k_end, zpad], axis=0).T
    decay_t = jnp.concatenate([jnp.exp(bl), zpad], axis=0).T
    row8 = lax.broadcasted_iota(jnp.int32, (SUBLANES, dv), 0)
    parts = []
    for g in range(rt // SUBLANES):
        q8 = q_dec[g * SUBLANES:(g + 1) * SUBLANES]
        o8 = jnp.zeros((SUBLANES, dv), F32)
        for jj in range(per):
            i = g * per + jj
            s = s0_ref[i]
            o8 = jnp.where(row8 // ts == jj, _dot(q8, s.astype(BF16)), o8)
            s = _col(decay_t, i * ts, dv) * s
            for t in range(ts):
                r = i * ts + t
                s = s + _col(k_end_t, r, dv) * vf[r:r + 1, :]
            so_ref[i] = s
        parts.append(o8)
    o = o + jnp.concatenate(parts, axis=0)
    o_ref[...] = _gla_out(o, gn_ref, gr_ref)


def gla_sample_call(proj, gate_w, gate_b, gla_norm, state_gla, layer, lay, row0, bs, ts, heads):
    dk = lay.gla_kd // heads
    dv = lay.gla_vd // heads
    assert SUBLANES % ts == 0 and GLA_CHUNK % ts == 0 and dk == LANES
    nb = _div_tile(bs, 32 // ts, SUBLANES // ts)
    rt = nb * ts
    assert rt % SUBLANES == 0 and rt <= LANES and row0 % rt == 0
    rb = row0 // rt
    colk = lambda off: pl.BlockSpec((rt, dk), lambda h, i: (rb + i, off // dk + h))
    colv = lambda off: pl.BlockSpec((rt, dv), lambda h, i: (rb + i, off // dv + h))
    return pl.pallas_call(
        functools.partial(_gla_s_body, ts),
        grid=(heads, bs // nb),
        in_specs=[colk(lay.gq), colk(lay.gk), colv(lay.gv), colv(lay.gr),
                  pl.BlockSpec((rt, LANES), lambda h, i: (rb + i, lay.small // LANES + 1)),
                  pl.BlockSpec((LANES, dk), lambda h, i: (0, h)),
                  pl.BlockSpec((1, dk), lambda h, i: (0, h)),
                  pl.BlockSpec((1, dv), lambda h, i: (0, 0)),
                  pl.BlockSpec((None, nb, None, dk, dv), lambda h, i: (layer, i, h, 0, 0))],
        out_specs=(pl.BlockSpec((rt, dv), lambda h, i: (i, h)),
                   pl.BlockSpec((nb, None, dk, dv), lambda h, i: (i, h, 0, 0))),
        out_shape=(jax.ShapeDtypeStruct((bs * ts, lay.gla_vd), BF16),
                   jax.ShapeDtypeStruct((bs, heads, dk, dv), F32)),
        compiler_params=_params("parallel", "parallel"),
        name="gla_sample",
    )(proj, proj, proj, proj, proj, gate_w, gate_b, gla_norm, state_gla)


def _merge_body(nn, n_prompt_tiles, mla_w, conv_w, x_ref, am_p, am_s, ac_p, ac_s, ag_p, ag_s,
                g0_ref, g1_ref, g2_ref, wb_ref, wo_ref, gn_ref, xo_ref, hn_ref):
    i = pl.program_id(0)
    n = pl.program_id(1)

    @pl.when(n == 0)
    def _():
        xo_ref[...] = x_ref[...]

    is_prompt = i < n_prompt_tiles
    a_mla = jnp.where(is_prompt, am_p[...], am_s[...])
    a_conv = jnp.where(is_prompt, ac_p[...], ac_s[...])
    a_gla = jnp.where(is_prompt, ag_p[...], ag_s[...])
    br_mla = _dot(a_mla, wb_ref[:mla_w, :])
    br_conv = _dot(a_conv, wb_ref[mla_w:mla_w + conv_w, :])
    br_gla = _dot(a_gla, wb_ref[mla_w + conv_w:, :])
    mix = (jax.nn.sigmoid(g0_ref[...]) * br_mla + jax.nn.sigmoid(g1_ref[...]) * br_conv
           + jax.nn.sigmoid(g2_ref[...]) * br_gla)
    xo_ref[...] += _dot(mix.astype(BF16), wo_ref[...])

    @pl.when(n == nn - 1)
    def _():
        hn_ref[...] = _rms(xo_ref[...], gn_ref[...]).astype(BF16)


def merge_call(x, mla_p, mla_s, conv_p, conv_s, gla_p, gla_s, proj, w_branch, w_out, g_next, lay):
    m, d = x.shape
    mp, ms = mla_p.shape[0], mla_s.shape[0]
    tm = _div_tile(ms, 512, 16)
    assert mp % tm == 0 and ms % tm == 0
    npt = mp // tm
    tn = 256
    nn = d // tn
    mla_w, conv_w, gla_w = mla_p.shape[1], conv_p.shape[1], gla_p.shape[1]
    row = pl.BlockSpec((tm, d), lambda i, n: (i, 0))
    act_p = lambda w: pl.BlockSpec((tm, w), lambda i, n: (jnp.minimum(i, npt - 1), 0))
    act_s = lambda w: pl.BlockSpec((tm, w), lambda i, n: (jnp.maximum(i - npt, 0), 0))
    gate = lambda k: pl.BlockSpec((tm, tn), lambda i, n: (i, lay.gates // tn + k * nn + n))
    return pl.pallas_call(
        functools.partial(_merge_body, nn, npt, mla_w, conv_w),
        grid=(m // tm, nn),
        in_specs=[row, act_p(mla_w), act_s(mla_w), act_p(conv_w), act_s(conv_w), act_p(gla_w), act_s(gla_w),
                  gate(0), gate(1), gate(2),
                  pl.BlockSpec((mla_w + conv_w + gla_w, tn), lambda i, n: (0, n)),
                  pl.BlockSpec((tn, d), lambda i, n: (n, 0)),
                  pl.BlockSpec((1, d), lambda i, n: (0, 0))],
        out_specs=(row, row),
        out_shape=(jax.ShapeDtypeStruct((m, d), F32), jax.ShapeDtypeStruct((m, d), BF16)),
        compiler_params=_params("parallel", "arbitrary"),
        name="merge",
    )(x, mla_p, mla_s, conv_p, conv_s, gla_p, gla_s, proj, proj, proj, w_branch, w_out, g_next.reshape(1, d))


def _rope_table(pos, rope):
    half = rope // 2
    inv = ROPE_THETA ** (-jnp.arange(half, dtype=F32) / half)
    ang = pos.astype(F32)[:, None] * inv[None, :]
    cos, sin = jnp.cos(ang), jnp.sin(ang)
    return jnp.concatenate([cos, cos, -sin, sin], axis=1)


def kernel(x_prompt, x_sample, cache_kv, cache_pe, state_conv, state_gla, page_table, norm_ffn1, ffn1_w_gate, ffn1_w_up, ffn1_w_down, norm_mix, w_in, q_norm, w_uq, kv_norm, w_uk, w_uv, conv_w, gla_gate_w, gla_gate_b, gla_norm, w_branch, w_out, norm_ffn2, ffn2_w_gate, ffn2_w_up, ffn2_w_down, norm_final):
    batch, seq, d = x_prompt.shape
    bs, ts, _ = x_sample.shape
    mp, ms = batch * seq, bs * ts
    depth = w_in.shape[0]
    q_lora, heads, qk = w_uq.shape[1:]
    kv_lora, _, nope = w_uk.shape[1:]
    rope = qk - nope
    vh = w_uv.shape[3]
    conv_dim = conv_w.shape[2]
    rank, gla_kd = gla_gate_w.shape[1:]
    g_heads, dk, dv = state_gla.shape[2:]
    gla_vd = g_heads * dv
    assert gla_kd == g_heads * dk and nope == LANES
    past_len = page_table.shape[1] * cache_kv.shape[2]
    scale = float(qk) ** -0.5
    lay = InLayout(q_lora, kv_lora, rope, conv_dim, gla_kd, gla_vd, rank, w_branch.shape[2] * 3)

    x = jnp.concatenate([x_prompt.reshape(mp, d), x_sample.reshape(ms, d)], axis=0)
    pos = jnp.concatenate([jnp.tile(jnp.arange(seq, dtype=jnp.int32), batch),
                           past_len + jnp.tile(jnp.arange(ts, dtype=jnp.int32), bs)])
    tab = _rope_table(pos, rope)
    half = rope // 2

    h = rmsnorm_call(x, norm_ffn1[0])
    outs = [[] for _ in range(8)]
    y = None
    for l in range(depth):
        x, h = ffn_call(x, h, ffn1_w_gate[l].astype(BF16), ffn1_w_up[l].astype(BF16),
                        ffn1_w_down[l].astype(BF16), norm_mix[l], False)
        proj, qn, kv, pe, kcat = inproj_call(h, lay.arrange(w_in[l]).astype(BF16), q_norm[l], kv_norm[l], tab, lay)

        wq = w_uq[l]
        wn = wq[:, :, :nope].reshape(q_lora, heads * nope).astype(BF16)
        wq_pe = wq[:, :, nope:]
        zeros = jnp.zeros_like(wq_pe)
        wp = jnp.concatenate([wq_pe, zeros], axis=2).reshape(q_lora, heads * LANES).astype(BF16)
        wps = jnp.concatenate([wq_pe[:, :, half:], wq_pe[:, :, :half], zeros], axis=2)
        wps = wps.reshape(q_lora, heads * LANES).astype(BF16)
        wuk_t = jnp.transpose(w_uk[l], (1, 2, 0)).astype(BF16)
        wuv = jnp.transpose(w_uv[l], (1, 0, 2)).astype(BF16)
        q = qproj_call(qn, wn, wp, wps, wuk_t, tab)
        mla_p = attn_prompt_call(q, kcat, wuv, batch, seq, scale)
        qs = q[:, mp:].reshape(heads, bs, ts, -1).transpose(1, 0, 2, 3).reshape(bs, heads * ts, -1)
        kn = kcat[mp:].reshape(bs, ts, -1)
        o_s = attn_sample_call(qs, kn, cache_kv, cache_pe, page_table, l, scale)
        o_t = o_s.reshape(bs, heads, ts, kv_lora).transpose(1, 0, 2, 3).reshape(heads, ms, kv_lora)
        mla_s = uv_call(o_t, wuv)

        conv_p, cst_p = conv_prompt_call(proj, conv_w[l], lay, batch, seq)
        state_rows = jnp.repeat(state_conv[l].reshape(bs, 2 * conv_dim), ts, axis=0)
        conv_s, u_s = conv_sample_call(proj, conv_w[l], state_rows, lay, mp, ts)
        cst_s = u_s.reshape(bs, ts, conv_dim)[:, ts - 2:]

        gw = jnp.zeros((LANES, gla_kd), BF16).at[:rank].set(gla_gate_w[l].astype(BF16))
        gb = gla_gate_b[l].reshape(1, gla_kd)
        gn = gla_norm[l].reshape(1, dv)
        gla_p, gst_p = gla_prompt_call(proj, gw, gb, gn, lay, batch, seq, g_heads)
        gla_s, gst_s = gla_sample_call(proj, gw, gb, gn, state_gla, l, lay, mp, bs, ts, g_heads)

        x, h = merge_call(x, mla_p, mla_s, conv_p, conv_s, gla_p, gla_s, proj,
                          w_branch[l].astype(BF16), w_out[l].astype(BF16), norm_ffn2[l], lay)
        last = l == depth - 1
        res = ffn_call(x, h, ffn2_w_gate[l].astype(BF16), ffn2_w_up[l].astype(BF16),
                       ffn2_w_down[l].astype(BF16), norm_final if last else norm_ffn1[l + 1], last)
        if last:
            y = res
        else:
            x, h = res

        for acc, val in zip(outs, (kv[:mp].reshape(batch, seq, kv_lora), pe[:mp].reshape(batch, seq, rope),
                                   cst_p, gst_p,
                                   kv[mp:].reshape(bs, ts, kv_lora), pe[mp:].reshape(bs, ts, rope),
                                   cst_s, gst_s)):
            acc.append(val)

    y_prompt = y[:mp].reshape(batch, seq, d)
    y_sample = y[mp:].reshape(bs, ts, d)
    return (y_prompt, y_sample) + tuple(jnp.stack(o) for o in outs)
```

```python
import functools
import math

import jax
import jax.numpy as jnp
import numpy as np
from jax import lax
from jax.experimental import pallas as pl
from jax.experimental.pallas import tpu as pltpu

F32 = jnp.float32
BF16 = jnp.bfloat16

EPS = 1e-6
ROPE_THETA = 10000.0
GLA_TAU = 16.0
GLA_CHUNK = 32
LOG2E = math.log2(math.e)
LANES = 128
SUBLANES = 8
VMEM_LIMIT = 56 * 1024 * 1024

NT_DIMS = (((1,), (1,)), ((), ()))


def _div_tile(n, target, mult):
    for t in range(min(n, target), 0, -1):
        if n % t == 0 and t % mult == 0:
            return t
    return n


def _params(*sem):
    return pltpu.CompilerParams(dimension_semantics=sem, vmem_limit_bytes=VMEM_LIMIT)


def _rms(x, g):
    return x * lax.rsqrt(jnp.mean(x * x, axis=-1, keepdims=True) + EPS) * g


def _dot(a, b):
    return jnp.dot(a, b, preferred_element_type=F32)


def _dot_nt(a, b):
    return lax.dot_general(a, b, NT_DIMS, preferred_element_type=F32)


def _log_sigmoid(x):
    return jnp.minimum(x, 0.0) - jnp.log1p(jnp.exp(-jnp.abs(x)))


def _rmsnorm_body(x_ref, g_ref, o_ref):
    o_ref[...] = _rms(x_ref[...], g_ref[...]).astype(o_ref.dtype)


def rmsnorm_call(x, g):
    m, d = x.shape
    tm = _div_tile(m, 512, 16)
    return pl.pallas_call(
        _rmsnorm_body,
        grid=(m // tm,),
        in_specs=[pl.BlockSpec((tm, d), lambda i: (i, 0)),
                  pl.BlockSpec((1, d), lambda i: (0, 0))],
        out_specs=pl.BlockSpec((tm, d), lambda i: (i, 0)),
        out_shape=jax.ShapeDtypeStruct((m, d), BF16),
        compiler_params=_params("parallel"),
        name="rmsnorm",
    )(x, g.reshape(1, d))


def _ffn_body(nf, final, x_ref, h_ref, wg_ref, wu_ref, wd_ref, g_ref, *rest):
    if final:
        y_ref, acc_ref = rest
    else:
        xo_ref, hn_ref, acc_ref = rest
    f = pl.program_id(1)

    @pl.when(f == 0)
    def _():
        acc_ref[...] = jnp.zeros_like(acc_ref)

    h = h_ref[...]
    gate = _dot(h, wg_ref[...])
    up = _dot(h, wu_ref[...])
    act = (gate * jax.nn.sigmoid(gate)) * up
    acc_ref[...] += _dot(act.astype(BF16), wd_ref[...])

    @pl.when(f == nf - 1)
    def _():
        xo = x_ref[...] + 0.5 * acc_ref[...]
        if final:
            y_ref[...] = _rms(xo, g_ref[...])
        else:
            xo_ref[...] = xo
            hn_ref[...] = _rms(xo, g_ref[...]).astype(BF16)


def ffn_call(x, h, wg, wu, wd, layer, g_next, final):
    m, d = x.shape
    ff = wg.shape[2]
    tm = _div_tile(m, 544, 16)
    tf = _div_tile(ff, 512, LANES)
    nf = ff // tf
    row = pl.BlockSpec((tm, d), lambda i, f: (i, 0))
    if final:
        out_shape = jax.ShapeDtypeStruct((m, d), F32)
        out_specs = row
    else:
        out_shape = (jax.ShapeDtypeStruct((m, d), F32), jax.ShapeDtypeStruct((m, d), BF16))
        out_specs = (row, row)
    return pl.pallas_call(
        functools.partial(_ffn_body, nf, final),
        grid=(m // tm, nf),
        in_specs=[row, row,
                  pl.BlockSpec((None, d, tf), lambda i, f: (layer, 0, f)),
                  pl.BlockSpec((None, d, tf), lambda i, f: (layer, 0, f)),
                  pl.BlockSpec((None, tf, d), lambda i, f: (layer, f, 0)),
                  pl.BlockSpec((1, d), lambda i, f: (0, 0))],
        out_specs=out_specs,
        out_shape=out_shape,
        scratch_shapes=[pltpu.VMEM((tm, d), F32)],
        compiler_params=_params("parallel", "arbitrary"),
        name="ffn",
    )(x, h, wg, wu, wd, g_next.reshape(1, d))


class InLayout:
    def __init__(self, q_lora, kv_lora, rope, conv, gla_kd, gla_vd, rank, n_gate):
        self.tn = 512
        assert q_lora == self.tn and kv_lora == self.tn and 2 * rope == LANES and rank <= LANES
        self.q_lora, self.kv_lora, self.rope = q_lora, kv_lora, rope
        self.conv, self.gla_kd, self.gla_vd, self.rank, self.n_gate = conv, gla_kd, gla_vd, rank, n_gate
        off = 0
        self.cq = off; off += q_lora
        self.ckv = off; off += kv_lora
        self.small = off; off += self.tn
        self.cb = off; off += conv
        self.cc = off; off += conv
        self.cv = off; off += conv
        self.gq = off; off += gla_kd
        self.gk = off; off += gla_kd
        self.gv = off; off += gla_vd
        self.gr = off; off += gla_vd
        self.gates = off; off += n_gate
        self.width = off
        assert self.width % self.tn == 0

    def arrange(self, w_in):
        sizes = (self.q_lora, self.kv_lora, self.rope, self.conv, self.conv, self.conv,
                 self.gla_kd, self.gla_kd, self.gla_vd, self.rank, self.gla_vd, self.n_gate)
        assert sum(sizes) == w_in.shape[-1]
        starts = [0]
        for s in sizes:
            starts.append(starts[-1] + s)
        c_q, c_kv, k_pe, cb, cc, cv, gq, gk, gv, gg, gr, gates = [
            w_in[..., starts[i]:starts[i + 1]] for i in range(len(sizes))]
        half = self.rope // 2
        k_pe_sw = jnp.concatenate([k_pe[..., half:], k_pe[..., :half]], axis=-1)
        pad = jnp.zeros(w_in.shape[:-1] + (self.tn - 2 * self.rope - self.rank,), w_in.dtype)
        return jnp.concatenate([c_q, c_kv, k_pe, k_pe_sw, gg, pad, cb, cc, cv, gq, gk, gv, gr, gates], axis=-1)


def _inproj_body(rope, h_ref, w_ref, qg_ref, kvg_ref, tab_ref, proj_ref, qn_ref, kv_ref, pe_ref, kcat_ref):
    n = pl.program_id(1)
    proj_ref[...] = _dot(h_ref[...], w_ref[...])

    @pl.when(n == 0)
    def _():
        qn_ref[...] = _rms(proj_ref[...], qg_ref[...]).astype(BF16)

    @pl.when(n == 1)
    def _():
        kv = _rms(proj_ref[...], kvg_ref[...])
        kv_ref[...] = kv
        kcat_ref[:, :kv.shape[1]] = kv.astype(BF16)

    @pl.when(n == 2)
    def _():
        prod = proj_ref[:, :LANES] * tab_ref[...]
        pe = prod + pltpu.roll(prod, rope, 1)
        lane = lax.broadcasted_iota(jnp.int32, pe.shape, 1)
        pe_ref[...] = pe[:, :rope]
        kcat_ref[:, kv_ref.shape[1]:] = jnp.where(lane < rope, pe, 0.0).astype(BF16)


def inproj_call(h, w_al, layer, q_norm, kv_norm, tab, lay):
    m, d = h.shape
    tn = lay.tn
    tm = _div_tile(m, 1088, 16)
    kw = lay.kv_lora + LANES
    row = lambda width: pl.BlockSpec((tm, width), lambda i, n: (i, 0))
    return pl.pallas_call(
        functools.partial(_inproj_body, lay.rope),
        grid=(m // tm, lay.width // tn),
        in_specs=[row(d),
                  pl.BlockSpec((None, d, tn), lambda i, n: (layer, 0, n)),
                  pl.BlockSpec((1, tn), lambda i, n: (0, 0)),
                  pl.BlockSpec((1, tn), lambda i, n: (0, 0)),
                  row(LANES)],
        out_specs=(pl.BlockSpec((tm, tn), lambda i, n: (i, n)),
                   row(tn), row(tn), row(lay.rope), row(kw)),
        out_shape=(jax.ShapeDtypeStruct((m, lay.width), F32),
                   jax.ShapeDtypeStruct((m, tn), BF16),
                   jax.ShapeDtypeStruct((m, tn), F32),
                   jax.ShapeDtypeStruct((m, lay.rope), F32),
                   jax.ShapeDtypeStruct((m, kw), BF16)),
        compiler_params=_params("parallel", "arbitrary"),
        name="inproj",
    )(h, w_al, q_norm.reshape(1, tn), kv_norm.reshape(1, tn), tab)


def _qproj_body(heads, nope, rope, qn_ref, wn_ref, wp_ref, wps_ref, wuk_ref, tab_ref, q_ref):
    qn = qn_ref[...]
    tab = tab_ref[...]
    tab_s = pltpu.roll(tab, rope, 1)
    q_nope = _dot(qn, wn_ref[...]).astype(BF16)
    q_pe = _dot(qn, wp_ref[...])
    q_pe_sw = _dot(qn, wps_ref[...])
    lat = wuk_ref.shape[2]
    for hd in range(heads):
        q_ref[hd, :, :lat] = _dot(q_nope[:, hd * nope:(hd + 1) * nope], wuk_ref[hd]).astype(BF16)
        sl = slice(hd * LANES, (hd + 1) * LANES)
        q_ref[hd, :, lat:] = (q_pe[:, sl] * tab + q_pe_sw[:, sl] * tab_s).astype(BF16)


def qproj_call(qn, wn, wp, wps, wuk_t, tab):
    m, ql = qn.shape
    heads, nope, lat = wuk_t.shape
    rope = LANES // 2
    tm = _div_tile(m, 272, 16)
    full = lambda a: pl.BlockSpec(a.shape, lambda i: (0,) * a.ndim)
    return pl.pallas_call(
        functools.partial(_qproj_body, heads, nope, rope),
        grid=(m // tm,),
        in_specs=[pl.BlockSpec((tm, ql), lambda i: (i, 0)),
                  full(wn), full(wp), full(wps), full(wuk_t),
                  pl.BlockSpec((tm, LANES), lambda i: (i, 0))],
        out_specs=pl.BlockSpec((heads, tm, lat + LANES), lambda i: (0, i, 0)),
        out_shape=jax.ShapeDtypeStruct((heads, m, lat + LANES), BF16),
        compiler_params=_params("parallel"),
        name="qproj",
    )(qn, wn, wp, wps, wuk_t, tab)


def _attn_p_body(tq, tk, heads, hc, scale, qt_ref, kt_ref, q_ref, k_ref, wuv_ref, o_ref, m_ref, l_ref, acc_ref):
    sid = pl.program_id(1)
    qi = qt_ref[sid]
    ki = kt_ref[sid]
    k_last = (qi * tq + tq - 1) // tk
    lat = acc_ref.shape[1]
    qw = q_ref.shape[2]
    rows_c = hc * tq
    c_exp = scale * LOG2E

    @pl.when(ki == 0)
    def _():
        m_ref[...] = jnp.full_like(m_ref, -jnp.inf)
        l_ref[...] = jnp.zeros_like(l_ref)
        acc_ref[...] = jnp.zeros_like(acc_ref)

    def step(masked):
        k = k_ref[...]
        kv = k[:, :lat]

        def scores(c):
            return _dot_nt(q_ref[c * hc:(c + 1) * hc].reshape(rows_c, qw), k)

        if masked:
            r = lax.broadcasted_iota(jnp.int32, (rows_c, tk), 0)
            col = lax.broadcasted_iota(jnp.int32, (rows_c, tk), 1)
            keep = ki * tk + col <= qi * tq + lax.rem(r, tq)
        s_next = scores(0)
        for c in range(heads // hc):
            s = s_next
            if (c + 1) * hc < heads:
                s_next = scores(c + 1)
            if masked:
                s = jnp.where(keep, s, -jnp.inf)
            rs = slice(c * rows_c, (c + 1) * rows_c)
            m_prev = m_ref[rs]
            m_new = jnp.maximum(m_prev, jnp.max(s, axis=-1, keepdims=True))
            alpha = jnp.exp2((m_prev - m_new) * c_exp)
            p = jnp.exp2((s - m_new) * c_exp)
            l_ref[rs] = alpha * l_ref[rs] + jnp.sum(p, axis=-1, keepdims=True)
            acc_ref[rs] = alpha * acc_ref[rs] + _dot(p.astype(BF16), kv)
            m_ref[rs] = m_new

    @pl.when(ki < k_last)
    def _():
        step(False)

    @pl.when(ki == k_last)
    def _():
        step(True)
        vh = wuv_ref.shape[2]
        for hd in range(heads):
            rs = slice(hd * tq, (hd + 1) * tq)
            o = (acc_ref[rs] / l_ref[rs]).astype(BF16)
            o_ref[:, hd * vh:(hd + 1) * vh] = _dot(o, wuv_ref[hd]).astype(BF16)


def attn_prompt_call(q, kcat, wuv, batch, seq, scale):
    heads, _, qw = q.shape
    lat, vh = wuv.shape[1], wuv.shape[2]
    tq = _div_tile(seq, 128, 16)
    tk = _div_tile(seq, 512, tq)
    hc = _div_tile(heads, 2, 1)
    assert tk % tq == 0
    nq, nk = seq // tq, seq // tk
    pairs = [(i, j) for i in range(nq) for j in range((i * tq + tq - 1) // tk + 1)]
    qt = jnp.asarray(np.array([p[0] for p in pairs], np.int32))
    kt = jnp.asarray(np.array([p[1] for p in pairs], np.int32))
    grid_spec = pltpu.PrefetchScalarGridSpec(
        num_scalar_prefetch=2,
        grid=(batch, len(pairs)),
        in_specs=[pl.BlockSpec((heads, tq, qw), lambda b, s, qt, kt: (0, b * nq + qt[s], 0)),
                  pl.BlockSpec((tk, qw), lambda b, s, qt, kt: (b * nk + kt[s], 0)),
                  pl.BlockSpec(wuv.shape, lambda b, s, qt, kt: (0, 0, 0))],
        out_specs=pl.BlockSpec((tq, heads * vh), lambda b, s, qt, kt: (b * nq + qt[s], 0)),
        scratch_shapes=[pltpu.VMEM((heads * tq, 1), F32), pltpu.VMEM((heads * tq, 1), F32),
                        pltpu.VMEM((heads * tq, lat), F32)],
    )
    return pl.pallas_call(
        functools.partial(_attn_p_body, tq, tk, heads, hc, scale),
        grid_spec=grid_spec,
        out_shape=jax.ShapeDtypeStruct((batch * seq, heads * vh), BF16),
        compiler_params=_params("parallel", "arbitrary"),
        name="attn_prompt",
    )(qt, kt, q, kcat, wuv)


SAMPLE_PAGES = 32
SAMPLE_GROUP = 8


def _attn_s_body(pages, grp, ts, nj, scale, pt_ref, q_ref, kn_ref, *refs):
    del pt_ref
    kv_refs, pe_refs = refs[:pages], refs[pages:2 * pages]
    o_ref, m_ref, l_ref, acc_ref = refs[2 * pages:]
    j = pl.program_id(1)
    lat = acc_ref.shape[1]
    rope = pe_refs[0].shape[0]
    c_exp = scale * LOG2E
    q = q_ref[...]
    q_lat = q[:, :lat]
    q_pe = q[:, lat:lat + rope]

    @pl.when(j == 0)
    def _():
        qf = q.astype(F32)
        kn = kn_ref[...].astype(F32)
        t_row = lax.rem(lax.broadcasted_iota(jnp.int32, (q.shape[0], 1), 0), ts)
        sc = []
        for t in range(ts):
            s_t = jnp.sum(qf * kn[t:t + 1, :], axis=-1, keepdims=True)
            sc.append(jnp.where(t <= t_row, s_t, -jnp.inf))
        m = functools.reduce(jnp.maximum, sc)
        l = jnp.zeros_like(m)
        acc = jnp.zeros((q.shape[0], lat), F32)
        for t in range(ts):
            p_t = jnp.exp2((sc[t] - m) * c_exp)
            l = l + p_t
            acc = acc + p_t.astype(BF16).astype(F32) * kn[t:t + 1, :lat]
        m_ref[...] = m
        l_ref[...] = l
        acc_ref[...] = acc

    def scores(g):
        kvs = [kv_refs[g * grp + i][...].astype(BF16) for i in range(grp)]
        s = jnp.concatenate(
            [_dot_nt(q_lat, kvs[i]) + _dot(q_pe, pe_refs[g * grp + i][...].astype(BF16)) for i in range(grp)],
            axis=1)
        return s, kvs

    nxt = scores(0)
    m = m_ref[...]
    l = l_ref[...]
    acc = acc_ref[...]
    for g in range(pages // grp):
        s, kvs = nxt
        if (g + 1) * grp < pages:
            nxt = scores(g + 1)
        m_new = jnp.maximum(m, jnp.max(s, axis=-1, keepdims=True))
        alpha = jnp.exp2((m - m_new) * c_exp)
        p = jnp.exp2((s - m_new) * c_exp)
        l = alpha * l + jnp.sum(p, axis=-1, keepdims=True)
        p = p.astype(BF16)
        page = kvs[0].shape[0]
        pv = _dot(p[:, :page], kvs[0])
        for i in range(1, grp):
            pv = pv + _dot(p[:, i * page:(i + 1) * page], kvs[i])
        acc = alpha * acc + pv
        m = m_new
    m_ref[...] = m
    l_ref[...] = l
    acc_ref[...] = acc

    @pl.when(j == nj - 1)
    def _():
        o_ref[...] = (acc / l).astype(BF16)


def attn_sample_call(qs, kn, cache_kv, cache_pe_t, page_table, layer, scale):
    bs, rows, qw = qs.shape
    ts = kn.shape[1]
    n_pages = page_table.shape[1]
    page, lat = cache_kv.shape[2], cache_kv.shape[3]
    rope = cache_pe_t.shape[2]
    pages = _div_tile(n_pages, SAMPLE_PAGES, 1)
    grp = _div_tile(pages, SAMPLE_GROUP, 1)
    nj = n_pages // pages

    def page_spec(shape, i):
        return pl.BlockSpec((None, None) + shape, lambda b, j, pt: (layer, pt[b, j * pages + i], 0, 0))

    grid_spec = pltpu.PrefetchScalarGridSpec(
        num_scalar_prefetch=1,
        grid=(bs, nj),
        in_specs=[pl.BlockSpec((None, rows, qw), lambda b, j, pt: (b, 0, 0)),
                  pl.BlockSpec((None, ts, qw), lambda b, j, pt: (b, 0, 0))]
                 + [page_spec((page, lat), i) for i in range(pages)]
                 + [page_spec((rope, page), i) for i in range(pages)],
        out_specs=pl.BlockSpec((None, rows, lat), lambda b, j, pt: (b, 0, 0)),
        scratch_shapes=[pltpu.VMEM((rows, 1), F32), pltpu.VMEM((rows, 1), F32), pltpu.VMEM((rows, lat), F32)],
    )
    return pl.pallas_call(
        functools.partial(_attn_s_body, pages, grp, ts, nj, scale),
        grid_spec=grid_spec,
        out_shape=jax.ShapeDtypeStruct((bs, rows, lat), BF16),
        compiler_params=_params("parallel", "arbitrary"),
        name="attn_sample",
    )(page_table, qs, kn, *([cache_kv] * pages), *([cache_pe_t] * pages))


def _uv_body(o_ref, w_ref, out_ref):
    out_ref[...] = _dot(o_ref[...], w_ref[...]).astype(BF16)


def uv_call(o_t, wuv):
    heads, m, lat = o_t.shape
    vh = wuv.shape[2]
    return pl.pallas_call(
        _uv_body,
        grid=(heads,),
        in_specs=[pl.BlockSpec((None, m, lat), lambda h: (h, 0, 0)),
                  pl.BlockSpec((None, lat, vh), lambda h: (h, 0, 0))],
        out_specs=pl.BlockSpec((m, vh), lambda h: (0, h)),
        out_shape=jax.ShapeDtypeStruct((m, heads * vh), BF16),
        compiler_params=_params("parallel"),
        name="uv_sample",
    )(o_t, wuv)


def _conv_taps(w_ref, u2, u1, u0):
    return w_ref[0:1, :] * u2 + w_ref[1:2, :] * u1 + w_ref[2:3, :] * u0


def _conv_p_body(nt, cb_ref, cc_ref, cv_ref, w_ref, o_ref, st_ref, carry_ref):
    ti = pl.program_id(2)

    @pl.when(ti == 0)
    def _():
        carry_ref[...] = jnp.zeros_like(carry_ref)

    u = cc_ref[...] * cv_ref[...]
    tt = u.shape[0]
    c1 = carry_ref[SUBLANES - 1:SUBLANES, :]
    c2 = carry_ref[SUBLANES - 2:SUBLANES - 1, :]
    row = lax.broadcasted_iota(jnp.int32, u.shape, 0)
    u1 = jnp.where(row == 0, c1, pltpu.roll(u, 1, 0))
    u2 = jnp.where(row == 0, c2, jnp.where(row == 1, c1, pltpu.roll(u, 2, 0)))
    o_ref[...] = (cb_ref[...] * _conv_taps(w_ref, u2, u1, u)).astype(BF16)
    carry_ref[...] = u[tt - SUBLANES:, :]

    @pl.when(ti == nt - 1)
    def _():
        st_ref[...] = u[tt - 2:, :]


def conv_prompt_call(proj, conv_w, lay, batch, seq):
    cdim = lay.conv
    tc = 512
    tt = _div_tile(seq, 512, SUBLANES)
    nt = seq // tt
    col = lambda off: pl.BlockSpec((tt, tc), lambda b, c, t: (b * nt + t, off // tc + c))
    return pl.pallas_call(
        functools.partial(_conv_p_body, nt),
        grid=(batch, cdim // tc, nt),
        in_specs=[col(lay.cb), col(lay.cc), col(lay.cv),
                  pl.BlockSpec((conv_w.shape[0], tc), lambda b, c, t: (0, c))],
        out_specs=(pl.BlockSpec((tt, tc), lambda b, c, t: (b * nt + t, c)),
                   pl.BlockSpec((None, 2, tc), lambda b, c, t: (b, 0, c))),
        out_shape=(jax.ShapeDtypeStruct((batch * seq, cdim), BF16),
                   jax.ShapeDtypeStruct((batch, 2, cdim), F32)),
        scratch_shapes=[pltpu.VMEM((SUBLANES, tc), F32)],
        compiler_params=_params("parallel", "parallel", "arbitrary"),
        name="conv_prompt",
    )(proj, proj, proj, conv_w)


def _conv_s_body(ts, cb_ref, cc_ref, cv_ref, w_ref, p0_ref, p1_ref, o_ref, u_ref):
    u = cc_ref[...] * cv_ref[...]
    p1 = p1_ref[...]
    tok = lax.rem(lax.broadcasted_iota(jnp.int32, u.shape, 0), ts)
    u1 = jnp.where(tok == 0, p1, pltpu.roll(u, 1, 0))
    u2 = jnp.where(tok == 0, p0_ref[...], jnp.where(tok == 1, p1, pltpu.roll(u, 2, 0)))
    o_ref[...] = (cb_ref[...] * _conv_taps(w_ref, u2, u1, u)).astype(BF16)
    u_ref[...] = u


def conv_sample_call(proj, conv_w, state_rows, lay, row0, ts):
    cdim = lay.conv
    tc = 512
    ms = state_rows.shape[0]
    assert row0 % ms == 0
    rb = row0 // ms
    nc = cdim // tc
    col = lambda off: pl.BlockSpec((ms, tc), lambda c: (rb, off // tc + c))
    st = lambda k: pl.BlockSpec((ms, tc), lambda c: (0, k * nc + c))
    out = pl.BlockSpec((ms, tc), lambda c: (0, c))
    return pl.pallas_call(
        functools.partial(_conv_s_body, ts),
        grid=(nc,),
        in_specs=[col(lay.cb), col(lay.cc), col(lay.cv),
                  pl.BlockSpec((conv_w.shape[0], tc), lambda c: (0, c)), st(0), st(1)],
        out_specs=(out, out),
        out_shape=(jax.ShapeDtypeStruct((ms, cdim), BF16), jax.ShapeDtypeStruct((ms, cdim), F32)),
        compiler_params=_params("parallel"),
        name="conv_sample",
    )(proj, proj, proj, conv_w, state_rows, state_rows)


def _seg_cumsum(x, pos, seg):
    d = 1
    while d < seg:
        x = x + jnp.where(pos >= d, pltpu.roll(x, d, 0), 0.0)
        d *= 2
    return x


def _gla_gates(gg_ref, gw_ref, gb_ref):
    lg = _dot(gg_ref[...].astype(BF16), gw_ref[...]) + gb_ref[...]
    return _log_sigmoid(lg) / GLA_TAU


def _gla_out(o, gn_ref, gr_ref):
    gr = gr_ref[...]
    return (_rms(o, gn_ref[...]) * (gr * jax.nn.sigmoid(gr))).astype(BF16)


def _col(xt, r, width):
    return jnp.broadcast_to(xt[:, r:r + 1], (xt.shape[0], width))


def _gla_p_body(nt, chunk, gq_ref, gk_ref, gv_ref, gr_ref, gg_ref, gw_ref, gb_ref, gn_ref, o_ref, so_ref, s_ref):
    ti = pl.program_id(2)

    @pl.when(ti == 0)
    def _():
        s_ref[...] = jnp.zeros_like(s_ref)

    tt, dk = gq_ref.shape
    dv = gv_ref.shape[1]
    nch = tt // chunk
    logf = _gla_gates(gg_ref, gw_ref, gb_ref)
    row = lax.broadcasted_iota(jnp.int32, (tt, dk), 0)
    b = _seg_cumsum(logf, lax.rem(row, chunk), chunk)
    b3 = b.reshape(nch, chunk, dk)
    bl = jnp.broadcast_to(b3[:, chunk - 1:chunk, :], b3.shape).reshape(tt, dk)
    k = gk_ref[...]
    q_dec = (gq_ref[...] * (dk ** -0.5) * jnp.exp(b)).astype(BF16)
    k_dec = (k * jnp.exp(-b)).astype(BF16)
    k_end = k * jnp.exp(bl - b)
    v = gv_ref[...].astype(BF16)

    r2 = lax.broadcasted_iota(jnp.int32, (tt, tt), 0)
    c2 = lax.broadcasted_iota(jnp.int32, (tt, tt), 1)
    same = (r2 // chunk == c2 // chunk) & (c2 <= r2)
    a = jnp.where(same, _dot_nt(q_dec, k_dec), 0.0)
    o = _dot(a.astype(BF16), v)

    k_end_t = k_end.T.astype(BF16)
    decay_t = jnp.exp(bl).T
    tok = lax.broadcasted_iota(jnp.int32, (dk, tt), 1)
    s = s_ref[...]
    parts = []
    for c in range(nch):
        lo = c * chunk
        parts.append(_dot(q_dec[lo:lo + chunk], s.astype(BF16)))
        k_c = jnp.where((tok >= lo) & (tok < lo + chunk), k_end_t, jnp.zeros_like(k_end_t))
        s = _col(decay_t, lo, dv) * s + _dot(k_c, v)
    s_ref[...] = s
    o = o + jnp.concatenate(parts, axis=0)
    o_ref[...] = _gla_out(o, gn_ref, gr_ref)

    @pl.when(ti == nt - 1)
    def _():
        so_ref[...] = s


def gla_prompt_call(proj, gate_w, gate_b, gla_norm, lay, batch, seq, heads):
    dk = lay.gla_kd // heads
    dv = lay.gla_vd // heads
    chunk = math.gcd(seq, GLA_CHUNK)
    tt = _div_tile(seq, 256, chunk)
    assert tt % LANES == 0 and dk % LANES == 0
    nt = seq // tt
    colk = lambda off: pl.BlockSpec((tt, dk), lambda b, h, t: (b * nt + t, off // dk + h))
    colv = lambda off: pl.BlockSpec((tt, dv), lambda b, h, t: (b * nt + t, off // dv + h))
    return pl.pallas_call(
        functools.partial(_gla_p_body, nt, chunk),
        grid=(batch, heads, nt),
        in_specs=[colk(lay.gq), colk(lay.gk), colv(lay.gv), colv(lay.gr),
                  pl.BlockSpec((tt, LANES), lambda b, h, t: (b * nt + t, lay.small // LANES + 1)),
                  pl.BlockSpec((LANES, dk), lambda b, h, t: (0, h)),
                  pl.BlockSpec((1, dk), lambda b, h, t: (0, h)),
                  pl.BlockSpec((1, dv), lambda b, h, t: (0, 0))],
        out_specs=(pl.BlockSpec((tt, dv), lambda b, h, t: (b * nt + t, h)),
                   pl.BlockSpec((None, None, dk, dv), lambda b, h, t: (b, h, 0, 0))),
        out_shape=(jax.ShapeDtypeStruct((batch * seq, lay.gla_vd), BF16),
                   jax.ShapeDtypeStruct((batch, heads, dk, dv), F32)),
        scratch_shapes=[pltpu.VMEM((dk, dv), F32)],
        compiler_params=_params("parallel", "parallel", "arbitrary"),
        name="gla_prompt",
    )(proj, proj, proj, proj, proj, gate_w, gate_b, gla_norm)


def _gla_s_body(ts, gq_ref, gk_ref, gv_ref, gr_ref, gg_ref, gw_ref, gb_ref, gn_ref, s0_ref, o_ref, so_ref):
    rt, dk = gq_ref.shape
    dv = gv_ref.shape[1]
    per = SUBLANES // ts
    logf = _gla_gates(gg_ref, gw_ref, gb_ref)
    row = lax.broadcasted_iota(jnp.int32, (rt, dk), 0)
    tok = lax.rem(row, ts)
    b = _seg_cumsum(logf, tok, ts)
    bl = b
    for d in range(1, ts):
        bl = jnp.where(tok == ts - 1 - d, pltpu.roll(b, rt - d, 0), bl)
    k = gk_ref[...]
    q_dec = (gq_ref[...] * (dk ** -0.5) * jnp.exp(b)).astype(BF16)
    k_dec = (k * jnp.exp(-b)).astype(BF16)
    k_end = (k * jnp.exp(bl - b)).astype(BF16).astype(F32)
    v16 = gv_ref[...].astype(BF16)
    vf = v16.astype(F32)

    r2 = lax.broadcasted_iota(jnp.int32, (rt, rt), 0)
    c2 = lax.broadcasted_iota(jnp.int32, (rt, rt), 1)
    same = (r2 // ts == c2 // ts) & (c2 <= r2)
    a = jnp.where(same, _dot_nt(q_dec, k_dec), 0.0)
    o = _dot(a.astype(BF16), v16)

    zpad = jnp.zeros((LANES - rt, dk), F32)
    k_end_t = jnp.concatenate([k_end, zpad], axis=0).T
    decay_t = jnp.concatenate([jnp.exp(bl), zpad], axis=0).T
    row8 = lax.broadcasted_iota(jnp.int32, (SUBLANES, dv), 0)
    parts = []
    for g in range(rt // SUBLANES):
        q8 = q_dec[g * SUBLANES:(g + 1) * SUBLANES]
        o8 = jnp.zeros((SUBLANES, dv), F32)
        for jj in range(per):
            i = g * per + jj
            s = s0_ref[i]
            o8 = jnp.where(row8 // ts == jj, _dot(q8, s.astype(BF16)), o8)
            s = _col(decay_t, i * ts, dv) * s
            for t in range(ts):
                r = i * ts + t
                s = s + _col(k_end_t, r, dv) * vf[r:r + 1, :]
            so_ref[i] = s
        parts.append(o8)
    o = o + jnp.concatenate(parts, axis=0)
    o_ref[...] = _gla_out(o, gn_ref, gr_ref)


def gla_sample_call(proj, gate_w, gate_b, gla_norm, state_gla, layer, lay, row0, bs, ts, heads):
    dk = lay.gla_kd // heads
    dv = lay.gla_vd // heads
    assert SUBLANES % ts == 0 and GLA_CHUNK % ts == 0 and dk == LANES
    nb = _div_tile(bs, 32 // ts, SUBLANES // ts)
    rt = nb * ts
    assert rt % SUBLANES == 0 and rt <= LANES and row0 % rt == 0
    rb = row0 // rt
    colk = lambda off: pl.BlockSpec((rt, dk), lambda h, i: (rb + i, off // dk + h))
    colv = lambda off: pl.BlockSpec((rt, dv), lambda h, i: (rb + i, off // dv + h))
    return pl.pallas_call(
        functools.partial(_gla_s_body, ts),
        grid=(heads, bs // nb),
        in_specs=[colk(lay.gq), colk(lay.gk), colv(lay.gv), colv(lay.gr),
                  pl.BlockSpec((rt, LANES), lambda h, i: (rb + i, lay.small // LANES + 1)),
                  pl.BlockSpec((LANES, dk), lambda h, i: (0, h)),
                  pl.BlockSpec((1, dk), lambda h, i: (0, h)),
                  pl.BlockSpec((1, dv), lambda h, i: (0, 0)),
                  pl.BlockSpec((None, nb, None, dk, dv), lambda h, i: (layer, i, h, 0, 0))],
        out_specs=(pl.BlockSpec((rt, dv), lambda h, i: (i, h)),
                   pl.BlockSpec((nb, None, dk, dv), lambda h, i: (i, h, 0, 0))),
        out_shape=(jax.ShapeDtypeStruct((bs * ts, lay.gla_vd), BF16),
                   jax.ShapeDtypeStruct((bs, heads, dk, dv), F32)),
        compiler_params=_params("parallel", "parallel"),
        name="gla_sample",
    )(proj, proj, proj, proj, proj, gate_w, gate_b, gla_norm, state_gla)


def _merge_body(nn, n_prompt_tiles, mla_w, conv_w, x_ref, am_p, am_s, ac_p, ac_s, ag_p, ag_s,
                g0_ref, g1_ref, g2_ref, wb_ref, wo_ref, gn_ref, xo_ref, hn_ref):
    i = pl.program_id(0)
    n = pl.program_id(1)

    @pl.when(n == 0)
    def _():
        xo_ref[...] = x_ref[...]

    is_prompt = i < n_prompt_tiles
    a_mla = jnp.where(is_prompt, am_p[...], am_s[...])
    a_conv = jnp.where(is_prompt, ac_p[...], ac_s[...])
    a_gla = jnp.where(is_prompt, ag_p[...], ag_s[...])
    br_mla = _dot(a_mla, wb_ref[:mla_w, :])
    br_conv = _dot(a_conv, wb_ref[mla_w:mla_w + conv_w, :])
    br_gla = _dot(a_gla, wb_ref[mla_w + conv_w:, :])
    mix = (jax.nn.sigmoid(g0_ref[...]) * br_mla + jax.nn.sigmoid(g1_ref[...]) * br_conv
           + jax.nn.sigmoid(g2_ref[...]) * br_gla)
    xo_ref[...] += _dot(mix.astype(BF16), wo_ref[...])

    @pl.when(n == nn - 1)
    def _():
        hn_ref[...] = _rms(xo_ref[...], gn_ref[...]).astype(BF16)


def merge_call(x, mla_p, mla_s, conv_p, conv_s, gla_p, gla_s, proj, w_branch, w_out, layer, g_next, lay):
    m, d = x.shape
    mp, ms = mla_p.shape[0], mla_s.shape[0]
    tm = _div_tile(ms, 512, 16)
    assert mp % tm == 0 and ms % tm == 0
    npt = mp // tm
    tn = 256
    nn = d // tn
    mla_w, conv_w, gla_w = mla_p.shape[1], conv_p.shape[1], gla_p.shape[1]
    row = pl.BlockSpec((tm, d), lambda i, n: (i, 0))
    act_p = lambda w: pl.BlockSpec((tm, w), lambda i, n: (jnp.minimum(i, npt - 1), 0))
    act_s = lambda w: pl.BlockSpec((tm, w), lambda i, n: (jnp.maximum(i - npt, 0), 0))
    gate = lambda k: pl.BlockSpec((tm, tn), lambda i, n: (i, lay.gates // tn + k * nn + n))
    return pl.pallas_call(
        functools.partial(_merge_body, nn, npt, mla_w, conv_w),
        grid=(m // tm, nn),
        in_specs=[row, act_p(mla_w), act_s(mla_w), act_p(conv_w), act_s(conv_w), act_p(gla_w), act_s(gla_w),
                  gate(0), gate(1), gate(2),
                  pl.BlockSpec((None, mla_w + conv_w + gla_w, tn), lambda i, n: (layer, 0, n)),
                  pl.BlockSpec((None, tn, d), lambda i, n: (layer, n, 0)),
                  pl.BlockSpec((1, d), lambda i, n: (0, 0))],
        out_specs=(row, row),
        out_shape=(jax.ShapeDtypeStruct((m, d), F32), jax.ShapeDtypeStruct((m, d), BF16)),
        compiler_params=_params("parallel", "arbitrary"),
        name="merge",
    )(x, mla_p, mla_s, conv_p, conv_s, gla_p, gla_s, proj, proj, proj, w_branch, w_out, g_next.reshape(1, d))


def _rope_table(pos, rope):
    half = rope // 2
    inv = ROPE_THETA ** (-jnp.arange(half, dtype=F32) / half)
    ang = pos.astype(F32)[:, None] * inv[None, :]
    cos, sin = jnp.cos(ang), jnp.sin(ang)
    return jnp.concatenate([cos, cos, -sin, sin], axis=1)


def kernel(x_prompt, x_sample, cache_kv, cache_pe, state_conv, state_gla, page_table, norm_ffn1, ffn1_w_gate, ffn1_w_up, ffn1_w_down, norm_mix, w_in, q_norm, w_uq, kv_norm, w_uk, w_uv, conv_w, gla_gate_w, gla_gate_b, gla_norm, w_branch, w_out, norm_ffn2, ffn2_w_gate, ffn2_w_up, ffn2_w_down, norm_final):
    batch, seq, d = x_prompt.shape
    bs, ts, _ = x_sample.shape
    mp, ms = batch * seq, bs * ts
    depth = w_in.shape[0]
    q_lora, heads, qk = w_uq.shape[1:]
    kv_lora, _, nope = w_uk.shape[1:]
    rope = qk - nope
    vh = w_uv.shape[3]
    conv_dim = conv_w.shape[2]
    rank, gla_kd = gla_gate_w.shape[1:]
    g_heads, dk, dv = state_gla.shape[2:]
    gla_vd = g_heads * dv
    assert gla_kd == g_heads * dk and nope == LANES
    past_len = page_table.shape[1] * cache_kv.shape[2]
    scale = float(qk) ** -0.5
    lay = InLayout(q_lora, kv_lora, rope, conv_dim, gla_kd, gla_vd, rank, w_branch.shape[2] * 3)

    x = jnp.concatenate([x_prompt.reshape(mp, d), x_sample.reshape(ms, d)], axis=0)
    pos = jnp.concatenate([jnp.tile(jnp.arange(seq, dtype=jnp.int32), batch),
                           past_len + jnp.tile(jnp.arange(ts, dtype=jnp.int32), bs)])
    tab = _rope_table(pos, rope)
    half = rope // 2

    assert ts >= 2
    f1 = [w.astype(BF16) for w in (ffn1_w_gate, ffn1_w_up, ffn1_w_down)]
    f2 = [w.astype(BF16) for w in (ffn2_w_gate, ffn2_w_up, ffn2_w_down)]
    wb16, wo16 = w_branch.astype(BF16), w_out.astype(BF16)
    w_al = lay.arrange(w_in.astype(BF16))
    cache_pe_t = jnp.swapaxes(cache_pe, 2, 3)

    h = rmsnorm_call(x, norm_ffn1[0])
    outs = [[] for _ in range(8)]
    y = None
    for l in range(depth):
        x, h = ffn_call(x, h, *f1, l, norm_mix[l], False)
        proj, qn, kv, pe, kcat = inproj_call(h, w_al, l, q_norm[l], kv_norm[l], tab, lay)

        wq = w_uq[l]
        wn = wq[:, :, :nope].reshape(q_lora, heads * nope).astype(BF16)
        wq_pe = wq[:, :, nope:]
        zeros = jnp.zeros_like(wq_pe)
        wp = jnp.concatenate([wq_pe, zeros], axis=2).reshape(q_lora, heads * LANES).astype(BF16)
        wps = jnp.concatenate([wq_pe[:, :, half:], wq_pe[:, :, :half], zeros], axis=2)
        wps = wps.reshape(q_lora, heads * LANES).astype(BF16)
        wuk_t = jnp.transpose(w_uk[l], (1, 2, 0)).astype(BF16)
        wuv = jnp.transpose(w_uv[l], (1, 0, 2)).astype(BF16)
        q = qproj_call(qn, wn, wp, wps, wuk_t, tab)
        mla_p = attn_prompt_call(q, kcat, wuv, batch, seq, scale)
        qs = q[:, mp:].reshape(heads, bs, ts, -1).transpose(1, 0, 2, 3).reshape(bs, heads * ts, -1)
        kn = kcat[mp:].reshape(bs, ts, -1)
        o_s = attn_sample_call(qs, kn, cache_kv, cache_pe_t, page_table, l, scale)
        o_t = o_s.reshape(bs, heads, ts, kv_lora).transpose(1, 0, 2, 3).reshape(heads, ms, kv_lora)
        mla_s = uv_call(o_t, wuv)

        conv_p, cst_p = conv_prompt_call(proj, conv_w[l], lay, batch, seq)
        state_rows = jnp.repeat(state_conv[l].reshape(bs, 2 * conv_dim), ts, axis=0)
        conv_s, u_s = conv_sample_call(proj, conv_w[l], state_rows, lay, mp, ts)
        cst_s = u_s.reshape(bs, ts, conv_dim)[:, ts - 2:]

        gw = jnp.zeros((LANES, gla_kd), BF16).at[:rank].set(gla_gate_w[l].astype(BF16))
        gb = gla_gate_b[l].reshape(1, gla_kd)
        gn = gla_norm[l].reshape(1, dv)
        gla_p, gst_p = gla_prompt_call(proj, gw, gb, gn, lay, batch, seq, g_heads)
        gla_s, gst_s = gla_sample_call(proj, gw, gb, gn, state_gla, l, lay, mp, bs, ts, g_heads)

        x, h = merge_call(x, mla_p, mla_s, conv_p, conv_s, gla_p, gla_s, proj, wb16, wo16, l, norm_ffn2[l], lay)
        last = l == depth - 1
        res = ffn_call(x, h, *f2, l, norm_final if last else norm_ffn1[l + 1], last)
        if last:
            y = res
        else:
            x, h = res

        for acc, val in zip(outs, (kv[:mp].reshape(batch, seq, kv_lora), pe[:mp].reshape(batch, seq, rope),
                                   cst_p, gst_p,
                                   kv[mp:].reshape(bs, ts, kv_lora), pe[mp:].reshape(bs, ts, rope),
                                   cst_s, gst_s)):
            acc.append(val)

    y_prompt = y[:mp].reshape(batch, seq, d)
    y_sample = y[mp:].reshape(bs, ts, d)
    return (y_prompt, y_sample) + tuple(jnp.stack(o) for o in outs)
```

```python
import functools
import math

import jax
import jax.numpy as jnp
import numpy as np
from jax import lax
from jax.experimental import pallas as pl
from jax.experimental.pallas import tpu as pltpu

F32 = jnp.float32
BF16 = jnp.bfloat16

EPS = 1e-6
ROPE_THETA = 10000.0
GLA_TAU = 16.0
GLA_CHUNK = 32
LOG2E = math.log2(math.e)
LANES = 128
SUBLANES = 8
VMEM_LIMIT = 56 * 1024 * 1024

NT_DIMS = (((1,), (1,)), ((), ()))


def _div_tile(n, target, mult):
    for t in range(min(n, target), 0, -1):
        if n % t == 0 and t % mult == 0:
            return t
    return n


def _params(*sem):
    return pltpu.CompilerParams(dimension_semantics=sem, vmem_limit_bytes=VMEM_LIMIT)


def _rms(x, g):
    return x * lax.rsqrt(jnp.mean(x * x, axis=-1, keepdims=True) + EPS) * g


def _dot(a, b):
    return jnp.dot(a, b, preferred_element_type=F32)


def _dot_nt(a, b):
    return lax.dot_general(a, b, NT_DIMS, preferred_element_type=F32)


def _log_sigmoid(x):
    return jnp.minimum(x, 0.0) - jnp.log1p(jnp.exp(-jnp.abs(x)))


def _rmsnorm_body(x_ref, g_ref, o_ref):
    o_ref[...] = _rms(x_ref[...], g_ref[...]).astype(o_ref.dtype)


def rmsnorm_call(x, g):
    m, d = x.shape
    tm = _div_tile(m, 512, 16)
    return pl.pallas_call(
        _rmsnorm_body,
        grid=(m // tm,),
        in_specs=[pl.BlockSpec((tm, d), lambda i: (i, 0)),
                  pl.BlockSpec((1, d), lambda i: (0, 0))],
        out_specs=pl.BlockSpec((tm, d), lambda i: (i, 0)),
        out_shape=jax.ShapeDtypeStruct((m, d), BF16),
        compiler_params=_params("parallel"),
        name="rmsnorm",
    )(x, g.reshape(1, d))


def _ffn_body(nf, final, x_ref, h_ref, wg_ref, wu_ref, wd_ref, g_ref, *rest):
    if final:
        y_ref, acc_ref = rest
    else:
        xo_ref, hn_ref, acc_ref = rest
    f = pl.program_id(1)

    @pl.when(f == 0)
    def _():
        acc_ref[...] = jnp.zeros_like(acc_ref)

    h = h_ref[...]
    gate = _dot(h, wg_ref[...])
    up = _dot(h, wu_ref[...])
    act = (gate * jax.nn.sigmoid(gate)) * up
    acc_ref[...] += _dot(act.astype(BF16), wd_ref[...])

    @pl.when(f == nf - 1)
    def _():
        xo = x_ref[...] + 0.5 * acc_ref[...]
        if final:
            y_ref[...] = _rms(xo, g_ref[...])
        else:
            xo_ref[...] = xo
            hn_ref[...] = _rms(xo, g_ref[...]).astype(BF16)


def ffn_call(x, h, wg, wu, wd, layer, g_next, final):
    m, d = x.shape
    ff = wg.shape[2]
    tm = _div_tile(m, 544, 16)
    tf = _div_tile(ff, 512, LANES)
    nf = ff // tf
    row = pl.BlockSpec((tm, d), lambda i, f: (i, 0))
    if final:
        out_shape = jax.ShapeDtypeStruct((m, d), F32)
        out_specs = row
    else:
        out_shape = (jax.ShapeDtypeStruct((m, d), F32), jax.ShapeDtypeStruct((m, d), BF16))
        out_specs = (row, row)
    return pl.pallas_call(
        functools.partial(_ffn_body, nf, final),
        grid=(m // tm, nf),
        in_specs=[row, row,
                  pl.BlockSpec((None, d, tf), lambda i, f: (layer, 0, f)),
                  pl.BlockSpec((None, d, tf), lambda i, f: (layer, 0, f)),
                  pl.BlockSpec((None, tf, d), lambda i, f: (layer, f, 0)),
                  pl.BlockSpec((1, d), lambda i, f: (0, 0))],
        out_specs=out_specs,
        out_shape=out_shape,
        scratch_shapes=[pltpu.VMEM((tm, d), F32)],
        compiler_params=_params("parallel", "arbitrary"),
        name="ffn",
    )(x, h, wg, wu, wd, g_next.reshape(1, d))


class InLayout:
    def __init__(self, q_lora, kv_lora, rope, conv, gla_kd, gla_vd, rank, n_gate):
        self.tn = 512
        assert q_lora == self.tn and kv_lora == self.tn and 2 * rope == LANES and rank <= LANES
        self.q_lora, self.kv_lora, self.rope = q_lora, kv_lora, rope
        self.conv, self.gla_kd, self.gla_vd, self.rank, self.n_gate = conv, gla_kd, gla_vd, rank, n_gate
        off = 0
        self.cq = off; off += q_lora
        self.ckv = off; off += kv_lora
        self.small = off; off += self.tn
        self.cb = off; off += conv
        self.cc = off; off += conv
        self.cv = off; off += conv
        self.gq = off; off += gla_kd
        self.gk = off; off += gla_kd
        self.gv = off; off += gla_vd
        self.gr = off; off += gla_vd
        self.gates = off; off += n_gate
        self.width = off
        assert self.width % self.tn == 0

    def arrange(self, w_in):
        sizes = (self.q_lora, self.kv_lora, self.rope, self.conv, self.conv, self.conv,
                 self.gla_kd, self.gla_kd, self.gla_vd, self.rank, self.gla_vd, self.n_gate)
        assert sum(sizes) == w_in.shape[-1]
        starts = [0]
        for s in sizes:
            starts.append(starts[-1] + s)
        c_q, c_kv, k_pe, cb, cc, cv, gq, gk, gv, gg, gr, gates = [
            w_in[..., starts[i]:starts[i + 1]] for i in range(len(sizes))]
        half = self.rope // 2
        k_pe_sw = jnp.concatenate([k_pe[..., half:], k_pe[..., :half]], axis=-1)
        pad = jnp.zeros(w_in.shape[:-1] + (self.tn - 2 * self.rope - self.rank,), w_in.dtype)
        return jnp.concatenate([c_q, c_kv, k_pe, k_pe_sw, gg, pad, cb, cc, cv, gq, gk, gv, gr, gates], axis=-1)


def _inproj_body(rope, h_ref, w_ref, qg_ref, kvg_ref, tab_ref, proj_ref, qn_ref, kv_ref, pe_ref, kcat_ref):
    n = pl.program_id(1)
    proj_ref[...] = _dot(h_ref[...], w_ref[...])

    @pl.when(n == 0)
    def _():
        qn_ref[...] = _rms(proj_ref[...], qg_ref[...]).astype(BF16)

    @pl.when(n == 1)
    def _():
        kv = _rms(proj_ref[...], kvg_ref[...])
        kv_ref[...] = kv
        kcat_ref[:, :kv.shape[1]] = kv.astype(BF16)

    @pl.when(n == 2)
    def _():
        prod = proj_ref[:, :LANES] * tab_ref[...]
        pe = prod + pltpu.roll(prod, rope, 1)
        lane = lax.broadcasted_iota(jnp.int32, pe.shape, 1)
        pe_ref[...] = pe[:, :rope]
        kcat_ref[:, kv_ref.shape[1]:] = jnp.where(lane < rope, pe, 0.0).astype(BF16)


def inproj_call(h, w_al, layer, q_norm, kv_norm, tab, lay):
    m, d = h.shape
    tn = lay.tn
    tm = _div_tile(m, 1088, 16)
    kw = lay.kv_lora + LANES
    row = lambda width: pl.BlockSpec((tm, width), lambda i, n: (i, 0))
    return pl.pallas_call(
        functools.partial(_inproj_body, lay.rope),
        grid=(m // tm, lay.width // tn),
        in_specs=[row(d),
                  pl.BlockSpec((None, d, tn), lambda i, n: (layer, 0, n)),
                  pl.BlockSpec((1, tn), lambda i, n: (0, 0)),
                  pl.BlockSpec((1, tn), lambda i, n: (0, 0)),
                  row(LANES)],
        out_specs=(pl.BlockSpec((tm, tn), lambda i, n: (i, n)),
                   row(tn), row(tn), row(lay.rope), row(kw)),
        out_shape=(jax.ShapeDtypeStruct((m, lay.width), F32),
                   jax.ShapeDtypeStruct((m, tn), BF16),
                   jax.ShapeDtypeStruct((m, tn), F32),
                   jax.ShapeDtypeStruct((m, lay.rope), F32),
                   jax.ShapeDtypeStruct((m, kw), BF16)),
        compiler_params=_params("parallel", "arbitrary"),
        name="inproj",
    )(h, w_al, q_norm.reshape(1, tn), kv_norm.reshape(1, tn), tab)


def _qproj_body(heads, nope, rope, qn_ref, wn_ref, wp_ref, wps_ref, wuk_ref, tab_ref, q_ref):
    qn = qn_ref[...]
    tab = tab_ref[...]
    tab_s = pltpu.roll(tab, rope, 1)
    q_nope = _dot(qn, wn_ref[...]).astype(BF16)
    q_pe = _dot(qn, wp_ref[...])
    q_pe_sw = _dot(qn, wps_ref[...])
    lat = wuk_ref.shape[2]
    for hd in range(heads):
        q_ref[hd, :, :lat] = _dot(q_nope[:, hd * nope:(hd + 1) * nope], wuk_ref[hd]).astype(BF16)
        sl = slice(hd * LANES, (hd + 1) * LANES)
        q_ref[hd, :, lat:] = (q_pe[:, sl] * tab + q_pe_sw[:, sl] * tab_s).astype(BF16)


def qproj_call(qn, wn, wp, wps, wuk_t, tab):
    m, ql = qn.shape
    heads, nope, lat = wuk_t.shape
    rope = LANES // 2
    tm = _div_tile(m, 272, 16)
    full = lambda a: pl.BlockSpec(a.shape, lambda i: (0,) * a.ndim)
    return pl.pallas_call(
        functools.partial(_qproj_body, heads, nope, rope),
        grid=(m // tm,),
        in_specs=[pl.BlockSpec((tm, ql), lambda i: (i, 0)),
                  full(wn), full(wp), full(wps), full(wuk_t),
                  pl.BlockSpec((tm, LANES), lambda i: (i, 0))],
        out_specs=pl.BlockSpec((heads, tm, lat + LANES), lambda i: (0, i, 0)),
        out_shape=jax.ShapeDtypeStruct((heads, m, lat + LANES), BF16),
        compiler_params=_params("parallel"),
        name="qproj",
    )(qn, wn, wp, wps, wuk_t, tab)


def _attn_p_body(tq, tk, heads, hc, scale, qt_ref, kt_ref, q_ref, k_ref, wuv_ref, o_ref, m_ref, l_ref, acc_ref):
    sid = pl.program_id(1)
    qi = qt_ref[sid]
    ki = kt_ref[sid]
    k_last = (qi * tq + tq - 1) // tk
    lat = acc_ref.shape[1]
    qw = q_ref.shape[2]
    rows_c = hc * tq
    c_exp = scale * LOG2E

    @pl.when(ki == 0)
    def _():
        m_ref[...] = jnp.full_like(m_ref, -jnp.inf)
        l_ref[...] = jnp.zeros_like(l_ref)
        acc_ref[...] = jnp.zeros_like(acc_ref)

    def step(masked):
        k = k_ref[...]
        kv = k[:, :lat]

        def scores(c):
            return _dot_nt(q_ref[c * hc:(c + 1) * hc].reshape(rows_c, qw), k)

        if masked:
            r = lax.broadcasted_iota(jnp.int32, (rows_c, tk), 0)
            col = lax.broadcasted_iota(jnp.int32, (rows_c, tk), 1)
            keep = ki * tk + col <= qi * tq + lax.rem(r, tq)
        s_next = scores(0)
        for c in range(heads // hc):
            s = s_next
            if (c + 1) * hc < heads:
                s_next = scores(c + 1)
            if masked:
                s = jnp.where(keep, s, -jnp.inf)
            rs = slice(c * rows_c, (c + 1) * rows_c)
            m_prev = m_ref[rs]
            m_new = jnp.maximum(m_prev, jnp.max(s, axis=-1, keepdims=True))
            alpha = jnp.exp2((m_prev - m_new) * c_exp)
            p = jnp.exp2((s - m_new) * c_exp)
            l_ref[rs] = alpha * l_ref[rs] + jnp.sum(p, axis=-1, keepdims=True)
            acc_ref[rs] = alpha * acc_ref[rs] + _dot(p.astype(BF16), kv)
            m_ref[rs] = m_new

    @pl.when(ki < k_last)
    def _():
        step(False)

    @pl.when(ki == k_last)
    def _():
        step(True)
        vh = wuv_ref.shape[2]
        for hd in range(heads):
            rs = slice(hd * tq, (hd + 1) * tq)
            o = (acc_ref[rs] / l_ref[rs]).astype(BF16)
            o_ref[:, hd * vh:(hd + 1) * vh] = _dot(o, wuv_ref[hd]).astype(BF16)


def attn_prompt_call(q, kcat, wuv, batch, seq, scale):
    heads, _, qw = q.shape
    lat, vh = wuv.shape[1], wuv.shape[2]
    tq = _div_tile(seq, 256, 16)
    tk = _div_tile(seq, 512, tq)
    hc = 1
    assert tk % tq == 0
    nq, nk = seq // tq, seq // tk
    pairs = [(i, j) for i in range(nq) for j in range((i * tq + tq - 1) // tk + 1)]
    qt = jnp.asarray(np.array([p[0] for p in pairs], np.int32))
    kt = jnp.asarray(np.array([p[1] for p in pairs], np.int32))
    grid_spec = pltpu.PrefetchScalarGridSpec(
        num_scalar_prefetch=2,
        grid=(batch, len(pairs)),
        in_specs=[pl.BlockSpec((heads, tq, qw), lambda b, s, qt, kt: (0, b * nq + qt[s], 0)),
                  pl.BlockSpec((tk, qw), lambda b, s, qt, kt: (b * nk + kt[s], 0)),
                  pl.BlockSpec(wuv.shape, lambda b, s, qt, kt: (0, 0, 0))],
        out_specs=pl.BlockSpec((tq, heads * vh), lambda b, s, qt, kt: (b * nq + qt[s], 0)),
        scratch_shapes=[pltpu.VMEM((heads * tq, 1), F32), pltpu.VMEM((heads * tq, 1), F32),
                        pltpu.VMEM((heads * tq, lat), F32)],
    )
    return pl.pallas_call(
        functools.partial(_attn_p_body, tq, tk, heads, hc, scale),
        grid_spec=grid_spec,
        out_shape=jax.ShapeDtypeStruct((batch * seq, heads * vh), BF16),
        compiler_params=_params("parallel", "arbitrary"),
        name="attn_prompt",
    )(qt, kt, q, kcat, wuv)


SAMPLE_PAGES = 32
SAMPLE_GROUP = 16


def _attn_s_body(pages, grp, ts, nj, scale, pt_ref, q_ref, kn_ref, *refs):
    del pt_ref
    kv_refs, pe_refs = refs[:pages], refs[pages:2 * pages]
    o_ref, m_ref, l_ref, acc_ref = refs[2 * pages:]
    j = pl.program_id(1)
    lat = acc_ref.shape[1]
    rope = pe_refs[0].shape[0]
    c_exp = scale * LOG2E
    q = q_ref[...]
    q_lat = q[:, :lat]
    q_pe = q[:, lat:lat + rope]

    @pl.when(j == 0)
    def _():
        qf = q.astype(F32)
        kn = kn_ref[...].astype(F32)
        t_row = lax.rem(lax.broadcasted_iota(jnp.int32, (q.shape[0], 1), 0), ts)
        sc = []
        for t in range(ts):
            s_t = jnp.sum(qf * kn[t:t + 1, :], axis=-1, keepdims=True)
            sc.append(jnp.where(t <= t_row, s_t, -jnp.inf))
        m = functools.reduce(jnp.maximum, sc)
        l = jnp.zeros_like(m)
        acc = jnp.zeros((q.shape[0], lat), F32)
        for t in range(ts):
            p_t = jnp.exp2((sc[t] - m) * c_exp)
            l = l + p_t
            acc = acc + p_t.astype(BF16).astype(F32) * kn[t:t + 1, :lat]
        m_ref[...] = m
        l_ref[...] = l
        acc_ref[...] = acc

    def scores(g):
        kv = jnp.concatenate([kv_refs[g * grp + i][...].astype(BF16) for i in range(grp)], axis=0)
        pe = jnp.concatenate([pe_refs[g * grp + i][...].astype(BF16) for i in range(grp)], axis=1)
        return _dot_nt(q_lat, kv) + _dot(q_pe, pe), kv

    nxt = scores(0)
    m = m_ref[...]
    l = l_ref[...]
    acc = acc_ref[...]
    for g in range(pages // grp):
        s, kv = nxt
        if (g + 1) * grp < pages:
            nxt = scores(g + 1)
        m_new = jnp.maximum(m, jnp.max(s, axis=-1, keepdims=True))
        alpha = jnp.exp2((m - m_new) * c_exp)
        p = jnp.exp2((s - m_new) * c_exp)
        l = alpha * l + jnp.sum(p, axis=-1, keepdims=True)
        acc = alpha * acc + _dot(p.astype(BF16), kv)
        m = m_new
    m_ref[...] = m
    l_ref[...] = l
    acc_ref[...] = acc

    @pl.when(j == nj - 1)
    def _():
        o_ref[...] = (acc / l).astype(BF16)


def attn_sample_call(qs, kn, cache_kv, cache_pe_t, page_table, layer, scale):
    bs, rows, qw = qs.shape
    ts = kn.shape[1]
    n_pages = page_table.shape[1]
    page, lat = cache_kv.shape[2], cache_kv.shape[3]
    rope = cache_pe_t.shape[2]
    pages = _div_tile(n_pages, SAMPLE_PAGES, 1)
    grp = _div_tile(pages, SAMPLE_GROUP, 1)
    nj = n_pages // pages

    def page_spec(shape, i):
        return pl.BlockSpec((None, None) + shape, lambda b, j, pt: (layer, pt[b, j * pages + i], 0, 0))

    grid_spec = pltpu.PrefetchScalarGridSpec(
        num_scalar_prefetch=1,
        grid=(bs, nj),
        in_specs=[pl.BlockSpec((None, rows, qw), lambda b, j, pt: (b, 0, 0)),
                  pl.BlockSpec((None, ts, qw), lambda b, j, pt: (b, 0, 0))]
                 + [page_spec((page, lat), i) for i in range(pages)]
                 + [page_spec((rope, page), i) for i in range(pages)],
        out_specs=pl.BlockSpec((None, rows, lat), lambda b, j, pt: (b, 0, 0)),
        scratch_shapes=[pltpu.VMEM((rows, 1), F32), pltpu.VMEM((rows, 1), F32), pltpu.VMEM((rows, lat), F32)],
    )
    return pl.pallas_call(
        functools.partial(_attn_s_body, pages, grp, ts, nj, scale),
        grid_spec=grid_spec,
        out_shape=jax.ShapeDtypeStruct((bs, rows, lat), BF16),
        compiler_params=_params("parallel", "arbitrary"),
        name="attn_sample",
    )(page_table, qs, kn, *([cache_kv] * pages), *([cache_pe_t] * pages))


def _uv_body(o_ref, w_ref, out_ref):
    out_ref[...] = _dot(o_ref[...], w_ref[...]).astype(BF16)


def uv_call(o_t, wuv):
    heads, m, lat = o_t.shape
    vh = wuv.shape[2]
    return pl.pallas_call(
        _uv_body,
        grid=(heads,),
        in_specs=[pl.BlockSpec((None, m, lat), lambda h: (h, 0, 0)),
                  pl.BlockSpec((None, lat, vh), lambda h: (h, 0, 0))],
        out_specs=pl.BlockSpec((m, vh), lambda h: (0, h)),
        out_shape=jax.ShapeDtypeStruct((m, heads * vh), BF16),
        compiler_params=_params("parallel"),
        name="uv_sample",
    )(o_t, wuv)


def _conv_taps(w_ref, u2, u1, u0):
    return w_ref[0:1, :] * u2 + w_ref[1:2, :] * u1 + w_ref[2:3, :] * u0


def _conv_p_body(nt, cb_ref, cc_ref, cv_ref, w_ref, o_ref, st_ref, carry_ref):
    ti = pl.program_id(2)

    @pl.when(ti == 0)
    def _():
        carry_ref[...] = jnp.zeros_like(carry_ref)

    u = cc_ref[...] * cv_ref[...]
    tt = u.shape[0]
    c1 = carry_ref[SUBLANES - 1:SUBLANES, :]
    c2 = carry_ref[SUBLANES - 2:SUBLANES - 1, :]
    row = lax.broadcasted_iota(jnp.int32, u.shape, 0)
    u1 = jnp.where(row == 0, c1, pltpu.roll(u, 1, 0))
    u2 = jnp.where(row == 0, c2, jnp.where(row == 1, c1, pltpu.roll(u, 2, 0)))
    o_ref[...] = (cb_ref[...] * _conv_taps(w_ref, u2, u1, u)).astype(BF16)
    carry_ref[...] = u[tt - SUBLANES:, :]

    @pl.when(ti == nt - 1)
    def _():
        st_ref[...] = u[tt - 2:, :]


def conv_prompt_call(proj, conv_w, lay, batch, seq):
    cdim = lay.conv
    tc = 512
    tt = _div_tile(seq, 512, SUBLANES)
    nt = seq // tt
    col = lambda off: pl.BlockSpec((tt, tc), lambda b, c, t: (b * nt + t, off // tc + c))
    return pl.pallas_call(
        functools.partial(_conv_p_body, nt),
        grid=(batch, cdim // tc, nt),
        in_specs=[col(lay.cb), col(lay.cc), col(lay.cv),
                  pl.BlockSpec((conv_w.shape[0], tc), lambda b, c, t: (0, c))],
        out_specs=(pl.BlockSpec((tt, tc), lambda b, c, t: (b * nt + t, c)),
                   pl.BlockSpec((None, 2, tc), lambda b, c, t: (b, 0, c))),
        out_shape=(jax.ShapeDtypeStruct((batch * seq, cdim), BF16),
                   jax.ShapeDtypeStruct((batch, 2, cdim), F32)),
        scratch_shapes=[pltpu.VMEM((SUBLANES, tc), F32)],
        compiler_params=_params("parallel", "parallel", "arbitrary"),
        name="conv_prompt",
    )(proj, proj, proj, conv_w)


def _conv_s_body(ts, cb_ref, cc_ref, cv_ref, w_ref, p0_ref, p1_ref, o_ref, u_ref):
    u = cc_ref[...] * cv_ref[...]
    p1 = p1_ref[...]
    tok = lax.rem(lax.broadcasted_iota(jnp.int32, u.shape, 0), ts)
    u1 = jnp.where(tok == 0, p1, pltpu.roll(u, 1, 0))
    u2 = jnp.where(tok == 0, p0_ref[...], jnp.where(tok == 1, p1, pltpu.roll(u, 2, 0)))
    o_ref[...] = (cb_ref[...] * _conv_taps(w_ref, u2, u1, u)).astype(BF16)
    u_ref[...] = u


def conv_sample_call(proj, conv_w, state_rows, lay, row0, ts):
    cdim = lay.conv
    tc = 512
    ms = state_rows.shape[0]
    assert row0 % ms == 0
    rb = row0 // ms
    nc = cdim // tc
    col = lambda off: pl.BlockSpec((ms, tc), lambda c: (rb, off // tc + c))
    st = lambda k: pl.BlockSpec((ms, tc), lambda c: (0, k * nc + c))
    out = pl.BlockSpec((ms, tc), lambda c: (0, c))
    return pl.pallas_call(
        functools.partial(_conv_s_body, ts),
        grid=(nc,),
        in_specs=[col(lay.cb), col(lay.cc), col(lay.cv),
                  pl.BlockSpec((conv_w.shape[0], tc), lambda c: (0, c)), st(0), st(1)],
        out_specs=(out, out),
        out_shape=(jax.ShapeDtypeStruct((ms, cdim), BF16), jax.ShapeDtypeStruct((ms, cdim), F32)),
        compiler_params=_params("parallel"),
        name="conv_sample",
    )(proj, proj, proj, conv_w, state_rows, state_rows)


def _seg_cumsum(x, pos, seg):
    d = 1
    while d < seg:
        x = x + jnp.where(pos >= d, pltpu.roll(x, d, 0), 0.0)
        d *= 2
    return x


def _gla_gates(gg_ref, gw_ref, gb_ref):
    lg = _dot(gg_ref[...].astype(BF16), gw_ref[...]) + gb_ref[...]
    return _log_sigmoid(lg) / GLA_TAU


def _gla_out(o, gn_ref, gr_ref):
    gr = gr_ref[...]
    return (_rms(o, gn_ref[...]) * (gr * jax.nn.sigmoid(gr))).astype(BF16)


def _col(xt, r, width):
    return jnp.broadcast_to(xt[:, r:r + 1], (xt.shape[0], width))


def _gla_p_body(nt, chunk, gq_ref, gk_ref, gv_ref, gr_ref, gg_ref, gw_ref, gb_ref, gn_ref, o_ref, so_ref, s_ref):
    ti = pl.program_id(2)

    @pl.when(ti == 0)
    def _():
        s_ref[...] = jnp.zeros_like(s_ref)

    tt, dk = gq_ref.shape
    dv = gv_ref.shape[1]
    nch = tt // chunk
    logf = _gla_gates(gg_ref, gw_ref, gb_ref)
    row = lax.broadcasted_iota(jnp.int32, (tt, dk), 0)
    b = _seg_cumsum(logf, lax.rem(row, chunk), chunk)
    b3 = b.reshape(nch, chunk, dk)
    bl = jnp.broadcast_to(b3[:, chunk - 1:chunk, :], b3.shape).reshape(tt, dk)
    k = gk_ref[...]
    q_dec = (gq_ref[...] * (dk ** -0.5) * jnp.exp(b)).astype(BF16)
    k_dec = (k * jnp.exp(-b)).astype(BF16)
    k_end = k * jnp.exp(bl - b)
    v = gv_ref[...].astype(BF16)

    r2 = lax.broadcasted_iota(jnp.int32, (tt, tt), 0)
    c2 = lax.broadcasted_iota(jnp.int32, (tt, tt), 1)
    same = (r2 // chunk == c2 // chunk) & (c2 <= r2)
    a = jnp.where(same, _dot_nt(q_dec, k_dec), 0.0)
    o = _dot(a.astype(BF16), v)

    k_end_t = k_end.T.astype(BF16)
    decay_t = jnp.exp(bl).T
    tok = lax.broadcasted_iota(jnp.int32, (dk, tt), 1)
    s = s_ref[...]
    parts = []
    for c in range(nch):
        lo = c * chunk
        parts.append(_dot(q_dec[lo:lo + chunk], s.astype(BF16)))
        k_c = jnp.where((tok >= lo) & (tok < lo + chunk), k_end_t, jnp.zeros_like(k_end_t))
        s = _col(decay_t, lo, dv) * s + _dot(k_c, v)
    s_ref[...] = s
    o = o + jnp.concatenate(parts, axis=0)
    o_ref[...] = _gla_out(o, gn_ref, gr_ref)

    @pl.when(ti == nt - 1)
    def _():
        so_ref[...] = s


def gla_prompt_call(proj, gate_w, gate_b, gla_norm, lay, batch, seq, heads):
    dk = lay.gla_kd // heads
    dv = lay.gla_vd // heads
    chunk = math.gcd(seq, GLA_CHUNK)
    tt = _div_tile(seq, 256, chunk)
    assert tt % LANES == 0 and dk % LANES == 0
    nt = seq // tt
    colk = lambda off: pl.BlockSpec((tt, dk), lambda b, h, t: (b * nt + t, off // dk + h))
    colv = lambda off: pl.BlockSpec((tt, dv), lambda b, h, t: (b * nt + t, off // dv + h))
    return pl.pallas_call(
        functools.partial(_gla_p_body, nt, chunk),
        grid=(batch, heads, nt),
        in_specs=[colk(lay.gq), colk(lay.gk), colv(lay.gv), colv(lay.gr),
                  pl.BlockSpec((tt, LANES), lambda b, h, t: (b * nt + t, lay.small // LANES + 1)),
                  pl.BlockSpec((LANES, dk), lambda b, h, t: (0, h)),
                  pl.BlockSpec((1, dk), lambda b, h, t: (0, h)),
                  pl.BlockSpec((1, dv), lambda b, h, t: (0, 0))],
        out_specs=(pl.BlockSpec((tt, dv), lambda b, h, t: (b * nt + t, h)),
                   pl.BlockSpec((None, None, dk, dv), lambda b, h, t: (b, h, 0, 0))),
        out_shape=(jax.ShapeDtypeStruct((batch * seq, lay.gla_vd), BF16),
                   jax.ShapeDtypeStruct((batch, heads, dk, dv), F32)),
        scratch_shapes=[pltpu.VMEM((dk, dv), F32)],
        compiler_params=_params("parallel", "parallel", "arbitrary"),
        name="gla_prompt",
    )(proj, proj, proj, proj, proj, gate_w, gate_b, gla_norm)


def _gla_s_body(ts, gq_ref, gk_ref, gv_ref, gr_ref, gg_ref, gw_ref, gb_ref, gn_ref, s0_ref, o_ref, so_ref):
    rt, dk = gq_ref.shape
    dv = gv_ref.shape[1]
    per = SUBLANES // ts
    logf = _gla_gates(gg_ref, gw_ref, gb_ref)
    row = lax.broadcasted_iota(jnp.int32, (rt, dk), 0)
    tok = lax.rem(row, ts)
    b = _seg_cumsum(logf, tok, ts)
    bl = b
    for d in range(1, ts):
        bl = jnp.where(tok == ts - 1 - d, pltpu.roll(b, rt - d, 0), bl)
    k = gk_ref[...]
    q_dec = (gq_ref[...] * (dk ** -0.5) * jnp.exp(b)).astype(BF16)
    k_dec = (k * jnp.exp(-b)).astype(BF16)
    k_end = (k * jnp.exp(bl - b)).astype(BF16).astype(F32)
    v16 = gv_ref[...].astype(BF16)
    vf = v16.astype(F32)

    r2 = lax.broadcasted_iota(jnp.int32, (rt, rt), 0)
    c2 = lax.broadcasted_iota(jnp.int32, (rt, rt), 1)
    same = (r2 // ts == c2 // ts) & (c2 <= r2)
    a = jnp.where(same, _dot_nt(q_dec, k_dec), 0.0)
    o = _dot(a.astype(BF16), v16)

    zpad = jnp.zeros((LANES - rt, dk), F32)
    k_end_t = jnp.concatenate([k_end, zpad], axis=0).T
    decay_t = jnp.concatenate([jnp.exp(bl), zpad], axis=0).T
    row8 = lax.broadcasted_iota(jnp.int32, (SUBLANES, dv), 0)
    parts = []
    for g in range(rt // SUBLANES):
        q8 = q_dec[g * SUBLANES:(g + 1) * SUBLANES]
        o8 = jnp.zeros((SUBLANES, dv), F32)
        for jj in range(per):
            i = g * per + jj
            s = s0_ref[i]
            o8 = jnp.where(row8 // ts == jj, _dot(q8, s.astype(BF16)), o8)
            s = _col(decay_t, i * ts, dv) * s
            for t in range(ts):
                r = i * ts + t
                s = s + _col(k_end_t, r, dv) * vf[r:r + 1, :]
            so_ref[i] = s
        parts.append(o8)
    o = o + jnp.concatenate(parts, axis=0)
    o_ref[...] = _gla_out(o, gn_ref, gr_ref)


def gla_sample_call(proj, gate_w, gate_b, gla_norm, state_gla, layer, lay, row0, bs, ts, heads):
    dk = lay.gla_kd // heads
    dv = lay.gla_vd // heads
    assert SUBLANES % ts == 0 and GLA_CHUNK % ts == 0 and dk == LANES
    nb = _div_tile(bs, 32 // ts, SUBLANES // ts)
    rt = nb * ts
    assert rt % SUBLANES == 0 and rt <= LANES and row0 % rt == 0
    rb = row0 // rt
    colk = lambda off: pl.BlockSpec((rt, dk), lambda h, i: (rb + i, off // dk + h))
    colv = lambda off: pl.BlockSpec((rt, dv), lambda h, i: (rb + i, off // dv + h))
    return pl.pallas_call(
        functools.partial(_gla_s_body, ts),
        grid=(heads, bs // nb),
        in_specs=[colk(lay.gq), colk(lay.gk), colv(lay.gv), colv(lay.gr),
                  pl.BlockSpec((rt, LANES), lambda h, i: (rb + i, lay.small // LANES + 1)),
                  pl.BlockSpec((LANES, dk), lambda h, i: (0, h)),
                  pl.BlockSpec((1, dk), lambda h, i: (0, h)),
                  pl.BlockSpec((1, dv), lambda h, i: (0, 0)),
                  pl.BlockSpec((None, nb, None, dk, dv), lambda h, i: (layer, i, h, 0, 0))],
        out_specs=(pl.BlockSpec((rt, dv), lambda h, i: (i, h)),
                   pl.BlockSpec((nb, None, dk, dv), lambda h, i: (i, h, 0, 0))),
        out_shape=(jax.ShapeDtypeStruct((bs * ts, lay.gla_vd), BF16),
                   jax.ShapeDtypeStruct((bs, heads, dk, dv), F32)),
        compiler_params=_params("parallel", "parallel"),
        name="gla_sample",
    )(proj, proj, proj, proj, proj, gate_w, gate_b, gla_norm, state_gla)


def _merge_body(nn, n_prompt_tiles, mla_w, conv_w, x_ref, am_p, am_s, ac_p, ac_s, ag_p, ag_s,
                g0_ref, g1_ref, g2_ref, wb_ref, wo_ref, gn_ref, xo_ref, hn_ref):
    i = pl.program_id(0)
    n = pl.program_id(1)

    @pl.when(n == 0)
    def _():
        xo_ref[...] = x_ref[...]

    is_prompt = i < n_prompt_tiles
    a_mla = jnp.where(is_prompt, am_p[...], am_s[...])
    a_conv = jnp.where(is_prompt, ac_p[...], ac_s[...])
    a_gla = jnp.where(is_prompt, ag_p[...], ag_s[...])
    br_mla = _dot(a_mla, wb_ref[:mla_w, :])
    br_conv = _dot(a_conv, wb_ref[mla_w:mla_w + conv_w, :])
    br_gla = _dot(a_gla, wb_ref[mla_w + conv_w:, :])
    mix = (jax.nn.sigmoid(g0_ref[...]) * br_mla + jax.nn.sigmoid(g1_ref[...]) * br_conv
           + jax.nn.sigmoid(g2_ref[...]) * br_gla)
    xo_ref[...] += _dot(mix.astype(BF16), wo_ref[...])

    @pl.when(n == nn - 1)
    def _():
        hn_ref[...] = _rms(xo_ref[...], gn_ref[...]).astype(BF16)


def merge_call(x, mla_p, mla_s, conv_p, conv_s, gla_p, gla_s, proj, w_branch, w_out, layer, g_next, lay):
    m, d = x.shape
    mp, ms = mla_p.shape[0], mla_s.shape[0]
    tm = _div_tile(ms, 512, 16)
    assert mp % tm == 0 and ms % tm == 0
    npt = mp // tm
    tn = 256
    nn = d // tn
    mla_w, conv_w, gla_w = mla_p.shape[1], conv_p.shape[1], gla_p.shape[1]
    row = pl.BlockSpec((tm, d), lambda i, n: (i, 0))
    act_p = lambda w: pl.BlockSpec((tm, w), lambda i, n: (jnp.minimum(i, npt - 1), 0))
    act_s = lambda w: pl.BlockSpec((tm, w), lambda i, n: (jnp.maximum(i - npt, 0), 0))
    gate = lambda k: pl.BlockSpec((tm, tn), lambda i, n: (i, lay.gates // tn + k * nn + n))
    return pl.pallas_call(
        functools.partial(_merge_body, nn, npt, mla_w, conv_w),
        grid=(m // tm, nn),
        in_specs=[row, act_p(mla_w), act_s(mla_w), act_p(conv_w), act_s(conv_w), act_p(gla_w), act_s(gla_w),
                  gate(0), gate(1), gate(2),
                  pl.BlockSpec((None, mla_w + conv_w + gla_w, tn), lambda i, n: (layer, 0, n)),
                  pl.BlockSpec((None, tn, d), lambda i, n: (layer, n, 0)),
                  pl.BlockSpec((1, d), lambda i, n: (0, 0))],
        out_specs=(row, row),
        out_shape=(jax.ShapeDtypeStruct((m, d), F32), jax.ShapeDtypeStruct((m, d), BF16)),
        compiler_params=_params("parallel", "arbitrary"),
        name="merge",
    )(x, mla_p, mla_s, conv_p, conv_s, gla_p, gla_s, proj, proj, proj, w_branch, w_out, g_next.reshape(1, d))


def _rope_table(pos, rope):
    half = rope // 2
    inv = ROPE_THETA ** (-jnp.arange(half, dtype=F32) / half)
    ang = pos.astype(F32)[:, None] * inv[None, :]
    cos, sin = jnp.cos(ang), jnp.sin(ang)
    return jnp.concatenate([cos, cos, -sin, sin], axis=1)


def kernel(x_prompt, x_sample, cache_kv, cache_pe, state_conv, state_gla, page_table, norm_ffn1, ffn1_w_gate, ffn1_w_up, ffn1_w_down, norm_mix, w_in, q_norm, w_uq, kv_norm, w_uk, w_uv, conv_w, gla_gate_w, gla_gate_b, gla_norm, w_branch, w_out, norm_ffn2, ffn2_w_gate, ffn2_w_up, ffn2_w_down, norm_final):
    batch, seq, d = x_prompt.shape
    bs, ts, _ = x_sample.shape
    mp, ms = batch * seq, bs * ts
    depth = w_in.shape[0]
    q_lora, heads, qk = w_uq.shape[1:]
    kv_lora, _, nope = w_uk.shape[1:]
    rope = qk - nope
    vh = w_uv.shape[3]
    conv_dim = conv_w.shape[2]
    rank, gla_kd = gla_gate_w.shape[1:]
    g_heads, dk, dv = state_gla.shape[2:]
    gla_vd = g_heads * dv
    assert gla_kd == g_heads * dk and nope == LANES
    past_len = page_table.shape[1] * cache_kv.shape[2]
    scale = float(qk) ** -0.5
    lay = InLayout(q_lora, kv_lora, rope, conv_dim, gla_kd, gla_vd, rank, w_branch.shape[2] * 3)

    x = jnp.concatenate([x_prompt.reshape(mp, d), x_sample.reshape(ms, d)], axis=0)
    pos = jnp.concatenate([jnp.tile(jnp.arange(seq, dtype=jnp.int32), batch),
                           past_len + jnp.tile(jnp.arange(ts, dtype=jnp.int32), bs)])
    tab = _rope_table(pos, rope)
    half = rope // 2

    assert ts >= 2
    f1 = [w.astype(BF16) for w in (ffn1_w_gate, ffn1_w_up, ffn1_w_down)]
    f2 = [w.astype(BF16) for w in (ffn2_w_gate, ffn2_w_up, ffn2_w_down)]
    wb16, wo16 = w_branch.astype(BF16), w_out.astype(BF16)
    w_al = lay.arrange(w_in).astype(BF16)
    cache_pe_t = jnp.swapaxes(cache_pe, 2, 3)

    h = rmsnorm_call(x, norm_ffn1[0])
    outs = [[] for _ in range(8)]
    y = None
    for l in range(depth):
        x, h = ffn_call(x, h, *f1, l, norm_mix[l], False)
        proj, qn, kv, pe, kcat = inproj_call(h, w_al, l, q_norm[l], kv_norm[l], tab, lay)

        wq = w_uq[l]
        wn = wq[:, :, :nope].reshape(q_lora, heads * nope).astype(BF16)
        wq_pe = wq[:, :, nope:]
        zeros = jnp.zeros_like(wq_pe)
        wp = jnp.concatenate([wq_pe, zeros], axis=2).reshape(q_lora, heads * LANES).astype(BF16)
        wps = jnp.concatenate([wq_pe[:, :, half:], wq_pe[:, :, :half], zeros], axis=2)
        wps = wps.reshape(q_lora, heads * LANES).astype(BF16)
        wuk_t = jnp.transpose(w_uk[l], (1, 2, 0)).astype(BF16)
        wuv = jnp.transpose(w_uv[l], (1, 0, 2)).astype(BF16)
        q = qproj_call(qn, wn, wp, wps, wuk_t, tab)
        mla_p = attn_prompt_call(q, kcat, wuv, batch, seq, scale)
        qs = q[:, mp:].reshape(heads, bs, ts, -1).transpose(1, 0, 2, 3).reshape(bs, heads * ts, -1)
        kn = kcat[mp:].reshape(bs, ts, -1)
        o_s = attn_sample_call(qs, kn, cache_kv, cache_pe_t, page_table, l, scale)
        o_t = o_s.reshape(bs, heads, ts, kv_lora).transpose(1, 0, 2, 3).reshape(heads, ms, kv_lora)
        mla_s = uv_call(o_t, wuv)

        conv_p, cst_p = conv_prompt_call(proj, conv_w[l], lay, batch, seq)
        state_rows = jnp.repeat(state_conv[l].reshape(bs, 2 * conv_dim), ts, axis=0)
        conv_s, u_s = conv_sample_call(proj, conv_w[l], state_rows, lay, mp, ts)
        cst_s = u_s.reshape(bs, ts, conv_dim)[:, ts - 2:]

        gw = jnp.zeros((LANES, gla_kd), BF16).at[:rank].set(gla_gate_w[l].astype(BF16))
        gb = gla_gate_b[l].reshape(1, gla_kd)
        gn = gla_norm[l].reshape(1, dv)
        gla_p, gst_p = gla_prompt_call(proj, gw, gb, gn, lay, batch, seq, g_heads)
        gla_s, gst_s = gla_sample_call(proj, gw, gb, gn, state_gla, l, lay, mp, bs, ts, g_heads)

        x, h = merge_call(x, mla_p, mla_s, conv_p, conv_s, gla_p, gla_s, proj, wb16, wo16, l, norm_ffn2[l], lay)
        last = l == depth - 1
        res = ffn_call(x, h, *f2, l, norm_final if last else norm_ffn1[l + 1], last)
        if last:
            y = res
        else:
            x, h = res

        for acc, val in zip(outs, (kv[:mp].reshape(batch, seq, kv_lora), pe[:mp].reshape(batch, seq, rope),
                                   cst_p, gst_p,
                                   kv[mp:].reshape(bs, ts, kv_lora), pe[mp:].reshape(bs, ts, rope),
                                   cst_s, gst_s)):
            acc.append(val)

    y_prompt = y[:mp].reshape(batch, seq, d)
    y_sample = y[mp:].reshape(bs, ts, d)
    return (y_prompt, y_sample) + tuple(jnp.stack(o) for o in outs)
```
